```python
import jax, jax.numpy as jnp
from jax import lax
import numpy as np

D_MODEL = 4096
BATCH = 1
SEQ = 16384
DEPTH = 4

CHUNK = 64
N_MEM = 256
D_FF = 11 * D_MODEL // 8
A_GROUP_DIM = 128
A_WIDTH = D_MODEL // 2
A_GROUPS = A_WIDTH // A_GROUP_DIM
A_BLOCK = 128
B_HEAD_DIM = 128
B_WIDTH = D_MODEL // 2
B_HEADS = B_WIDTH // B_HEAD_DIM
QBLOCK = 128
AB_IN = 2 * A_WIDTH + 3 * B_WIDTH
AB_MIX = A_WIDTH + B_WIDTH
C_WIDTH = D_MODEL
C_KERNEL = 31
X_HEADS = 4
X_WIDTH = D_MODEL // 4
X_HEAD_DIM = X_WIDTH // X_HEADS
N_EVEN = (DEPTH + 1) // 2
N_ODD = DEPTH // 2
EPS = 1e-6

kernel_name = "hybrid_gmlp_stickbreak_conformer_trunk"


def rms_norm(x, g):
    xf = x.astype(jnp.float32)
    y = xf * lax.rsqrt(jnp.mean(xf * xf, axis=-1, keepdims=True) + EPS)
    return (y * g.astype(jnp.float32)).astype(x.dtype)


def layer_norm(x, g, b):
    xf = x.astype(jnp.float32)
    mu = jnp.mean(xf, axis=-1, keepdims=True)
    xc = xf - mu
    y = xc * lax.rsqrt(jnp.mean(xc * xc, axis=-1, keepdims=True) + EPS)
    return (y * g.astype(jnp.float32) + b.astype(jnp.float32)).astype(x.dtype)


def swiglu(x, w_gate, w_up, w_down):
    return (jax.nn.silu(x @ w_gate) * (x @ w_up)) @ w_down


def chunked_spatial_gating(u, v, ln_g, ln_b, w_s, b_s):
    bsz, s_len, _ = v.shape
    v = layer_norm(v, ln_g, ln_b)
    pos = jnp.arange(A_BLOCK)
    mask = (pos[None, :] // CHUNK) <= (pos[:, None] // CHUNK)
    w = jnp.where(mask[None], w_s, jnp.zeros((), w_s.dtype))
    vb = v.reshape(bsz, s_len // A_BLOCK, A_BLOCK, A_GROUPS, A_GROUP_DIM)
    mixed = jnp.einsum('gts,bcsgd->bctgd', w, vb) + b_s.T[None, None, :, :, None]
    return u * mixed.reshape(bsz, s_len, A_WIDTH)


def _strict_lower(n):
    return jnp.asarray(np.tril(np.ones((n, n), np.float32), -1))


def stick_breaking_attention(q, k, v):
    bsz, s_len, n_h, d_h = q.shape
    n_blocks = s_len // QBLOCK
    scale = d_h ** -0.5
    low_in = _strict_lower(QBLOCK)
    outs = []
    for i in range(n_blocks):
        nk = i + 1
        klen = nk * QBLOCK
        qb = q[:, i * QBLOCK:(i + 1) * QBLOCK]
        kb = k[:, :klen]
        vb = v[:, :klen]
        z = jnp.einsum('bqhd,bkhd->bhqk', qb, kb,
                       preferred_element_type=jnp.float32) * scale
        qpos = i * QBLOCK + jnp.arange(QBLOCK)
        kpos = jnp.arange(klen)
        mask = kpos[None, :] < qpos[:, None]
        log_keep = jnp.where(mask, jax.nn.log_sigmoid(-z), 0.0)
        lk = log_keep.reshape(bsz, n_h, QBLOCK, nk, QBLOCK)
        within = jnp.einsum('bhqnj,jc->bhqnc', lk, low_in)
        after = jnp.einsum('bhqj,jc->bhqc', lk.sum(-1), _strict_lower(nk))
        later = (within + after[..., None]).reshape(bsz, n_h, QBLOCK, klen)
        a = jnp.where(mask, jnp.exp(jax.nn.log_sigmoid(z) + later), 0.0)
        outs.append(jnp.einsum('bhqk,bkhd->bqhd', a.astype(v.dtype), vb))
    return jnp.concatenate(outs, axis=1).reshape(bsz, s_len, n_h * d_h)


def mixer_ab(h, w_in, w_out, ln_g, ln_b, w_s, b_s):
    bsz, s_len, _ = h.shape
    p = h @ w_in
    u, v, q, k, vv = jnp.split(
        p, [A_WIDTH, 2 * A_WIDTH, 2 * A_WIDTH + B_WIDTH, 2 * A_WIDTH + 2 * B_WIDTH], axis=-1)
    a_out = chunked_spatial_gating(jax.nn.gelu(u), jax.nn.gelu(v), ln_g, ln_b, w_s, b_s)
    heads = lambda t: t.reshape(bsz, s_len, B_HEADS, B_HEAD_DIM)
    b_out = stick_breaking_attention(heads(q), heads(k), heads(vv))
    return jnp.concatenate([a_out, b_out], axis=-1) @ w_out


def causal_depthwise_conv(x, w, b):
    y = lax.conv_general_dilated(
        x, w[:, None, :], window_strides=(1,), padding=[(C_KERNEL - 1, 0)],
        dimension_numbers=('NWC', 'WIO', 'NWC'), feature_group_count=x.shape[-1])
    return y + b


def conformer_conv(h, w_in, w_dw, b_dw, ln_g, ln_b, w_out):
    a, g = jnp.split(h @ w_in, 2, axis=-1)
    y = a * jax.nn.sigmoid(g)
    y = causal_depthwise_conv(y, w_dw, b_dw)
    y = jax.nn.silu(layer_norm(y, ln_g, ln_b))
    return y @ w_out


def memory_cross_attention(h, m, w_q, w_k, w_v, w_o):
    bsz, s_len, _ = h.shape
    n_m = m.shape[1]
    q = (h @ w_q).reshape(bsz, s_len, X_HEADS, X_HEAD_DIM)
    k = (m @ w_k).reshape(bsz, n_m, X_HEADS, X_HEAD_DIM)
    v = (m @ w_v).reshape(bsz, n_m, X_HEADS, X_HEAD_DIM)
    s = jnp.einsum('bqhd,bkhd->bhqk', q, k,
                   preferred_element_type=jnp.float32) * (X_HEAD_DIM ** -0.5)
    p = jax.nn.softmax(s, axis=-1)
    o = jnp.einsum('bhqk,bkhd->bqhd', p.astype(v.dtype), v).reshape(bsz, s_len, X_WIDTH)
    return o @ w_o


def setup_inputs(seed: int = 0) -> dict:
    key = jax.random.key(seed)
    keys = iter(jax.random.split(key, 64))

    def nrm(shape, scale):
        return jax.random.normal(next(keys), shape, jnp.float32) * scale

    def gain(shape):
        return 1.0 + nrm(shape, 0.05)

    D, L = D_MODEL, DEPTH
    return {
        "x": nrm((BATCH, SEQ, D), 1.0),
        "mem": nrm((BATCH, N_MEM, D), 1.0),
        "ffn1_pre_g": gain((L, D)),
        "ffn1_post_g": gain((L, D)),
        "ffn1_w_gate": nrm((L, D, D_FF), D ** -0.5),
        "ffn1_w_up": nrm((L, D, D_FF), D ** -0.5),
        "ffn1_w_down": nrm((L, D_FF, D), D_FF ** -0.5),
        "mix_pre_g": gain((L, D)),
        "mix_post_g": gain((L, D)),
        "ab_w_in": nrm((N_EVEN, D, AB_IN), D ** -0.5),
        "ab_w_out": nrm((N_EVEN, AB_MIX, D), AB_MIX ** -0.5),
        "a_ln_g": gain((N_EVEN, A_WIDTH)),
        "a_ln_b": nrm((N_EVEN, A_WIDTH), 0.01),
        "a_w_s": nrm((N_EVEN, A_GROUPS, A_BLOCK, A_BLOCK), A_BLOCK ** -0.5),
        "a_b_s": gain((N_EVEN, A_GROUPS, A_BLOCK)),
        "c_w_in": nrm((N_ODD, D, 2 * C_WIDTH), D ** -0.5),
        "c_w_dw": nrm((N_ODD, C_KERNEL, C_WIDTH), C_KERNEL ** -0.5),
        "c_b_dw": nrm((N_ODD, C_WIDTH), 0.01),
        "c_ln_g": gain((N_ODD, C_WIDTH)),
        "c_ln_b": nrm((N_ODD, C_WIDTH), 0.01),
        "c_w_out": nrm((N_ODD, C_WIDTH, D), C_WIDTH ** -0.5),
        "xa_pre_g": gain((L, D)),
        "xa_post_g": gain((L, D)),
        "xa_mem_g": gain((L, D)),
        "xa_w_q": nrm((L, D, X_WIDTH), D ** -0.5),
        "xa_w_k": nrm((L, D, X_WIDTH), D ** -0.5),
        "xa_w_v": nrm((L, D, X_WIDTH), D ** -0.5),
        "xa_w_o": nrm((L, X_WIDTH, D), X_WIDTH ** -0.5),
        "ffn2_pre_g": gain((L, D)),
        "ffn2_post_g": gain((L, D)),
        "ffn2_w_gate": nrm((L, D, D_FF), D ** -0.5),
        "ffn2_w_up": nrm((L, D, D_FF), D ** -0.5),
        "ffn2_w_down": nrm((L, D_FF, D), D_FF ** -0.5),
    }


def reference(x, mem,
              ffn1_pre_g, ffn1_post_g, ffn1_w_gate, ffn1_w_up, ffn1_w_down,
              mix_pre_g, mix_post_g,
              ab_w_in, ab_w_out, a_ln_g, a_ln_b, a_w_s, a_b_s,
              c_w_in, c_w_dw, c_b_dw, c_ln_g, c_ln_b, c_w_out,
              xa_pre_g, xa_post_g, xa_mem_g, xa_w_q, xa_w_k, xa_w_v, xa_w_o,
              ffn2_pre_g, ffn2_post_g, ffn2_w_gate, ffn2_w_up, ffn2_w_down):
    for l in range(DEPTH):
        h = swiglu(rms_norm(x, ffn1_pre_g[l]), ffn1_w_gate[l], ffn1_w_up[l], ffn1_w_down[l])
        x = x + 0.5 * rms_norm(h, ffn1_post_g[l])
        h = rms_norm(x, mix_pre_g[l])
        if l % 2 == 0:
            e = l // 2
            h = mixer_ab(h, ab_w_in[e], ab_w_out[e], a_ln_g[e], a_ln_b[e], a_w_s[e], a_b_s[e])
        else:
            o = l // 2
            h = conformer_conv(h, c_w_in[o], c_w_dw[o], c_b_dw[o], c_ln_g[o], c_ln_b[o], c_w_out[o])
        x = x + rms_norm(h, mix_post_g[l])
        m = rms_norm(mem, xa_mem_g[l])
        h = memory_cross_attention(rms_norm(x, xa_pre_g[l]), m,
                                   xa_w_q[l], xa_w_k[l], xa_w_v[l], xa_w_o[l])
        x = x + rms_norm(h, xa_post_g[l])
        h = swiglu(rms_norm(x, ffn2_pre_g[l]), ffn2_w_gate[l], ffn2_w_up[l], ffn2_w_down[l])
        x = x + 0.5 * rms_norm(h, ffn2_post_g[l])
    return x
```

```python
import functools

import jax
import jax.numpy as jnp
from jax import lax
from jax.experimental import pallas as pl
from jax.experimental.pallas import tpu as pltpu

EPS = 1e-6
CHUNK = 64
A_BLOCK = 128
A_GROUP_DIM = 128
B_HEAD_DIM = 128
X_HEADS = 4
C_KERNEL = 31
C_HALO = 32

VMEM_LIMIT_BYTES = 56 * 1024 * 1024

BF16 = jnp.bfloat16
F32 = jnp.float32


def _params(*sem):
    return pltpu.CompilerParams(dimension_semantics=sem, vmem_limit_bytes=VMEM_LIMIT_BYTES)


def _mm_body(*refs, n_lhs, n_grp, epilogue, has_vec):
    o_ref = refs[-1]
    lhs = refs[:n_lhs]
    rhs = refs[n_lhs:n_lhs + n_lhs * n_grp]
    accs = []
    for g in range(n_grp):
        acc = None
        for l in range(n_lhs):
            d = jnp.dot(lhs[l][...], rhs[g * n_lhs + l][...], preferred_element_type=F32)
            acc = d if acc is None else acc + d
        accs.append(acc)
    if has_vec:
        accs.append(refs[n_lhs + n_lhs * n_grp][...])
    o_ref[...] = epilogue(*accs).astype(o_ref.dtype)


def matmul(lhs, rhs, epilogue, out_dtype, *, tm, tn, vec=None, name="matmul"):
    n_lhs, n_grp = len(lhs), len(rhs)
    m = lhs[0].shape[0]
    n = rhs[0][0].shape[1]
    tm, tn = min(tm, m), min(tn, n)
    assert m % tm == 0 and n % tn == 0
    in_specs = [pl.BlockSpec((tm, a.shape[1]), lambda i, j: (i, 0)) for a in lhs]
    flat_rhs = []
    for grp in rhs:
        assert len(grp) == n_lhs
        for a, w in zip(lhs, grp):
            assert w.shape == (a.shape[1], n)
            in_specs.append(pl.BlockSpec((w.shape[0], tn), lambda i, j: (0, j)))
            flat_rhs.append(w)
    args = list(lhs) + flat_rhs
    if vec is not None:
        in_specs.append(pl.BlockSpec((1, tn), lambda i, j: (0, j)))
        args.append(vec)
    body = functools.partial(_mm_body, n_lhs=n_lhs, n_grp=n_grp, epilogue=epilogue,
                             has_vec=vec is not None)
    return pl.pallas_call(
        body,
        grid=(m // tm, n // tn),
        in_specs=in_specs,
        out_specs=pl.BlockSpec((tm, tn), lambda i, j: (i, j)),
        out_shape=jax.ShapeDtypeStruct((m, n), out_dtype),
        compiler_params=_params("parallel", "arbitrary"),
        name=name,
    )(*args)


def _identity(a):
    return a


def _colscale(a, s):
    return a * s


def _swiglu_gate(g, u):
    return g * jax.nn.sigmoid(g) * u


def _glu_gate(a, g):
    return a * jax.nn.sigmoid(g)


def _gelu_tanh(a):
    return jax.nn.gelu(a, approximate=True)


def _rms(x, g):
    return x * lax.rsqrt(jnp.mean(x * x, axis=-1, keepdims=True) + EPS) * g


def _prenorm_body(x_ref, g_ref, o_ref):
    o_ref[...] = _rms(x_ref[...], g_ref[...]).astype(o_ref.dtype)


def prenorm(x, g, *, tm=256):
    m, d = x.shape
    tm = min(tm, m)
    return pl.pallas_call(
        _prenorm_body,
        grid=(m // tm,),
        in_specs=[pl.BlockSpec((tm, d), lambda i: (i, 0)), pl.BlockSpec((1, d), lambda i: (0, 0))],
        out_specs=pl.BlockSpec((tm, d), lambda i: (i, 0)),
        out_shape=jax.ShapeDtypeStruct((m, d), BF16),
        compiler_params=_params("parallel"),
        name="prenorm",
    )(x, g.reshape(1, d))


def _residual_body(h_ref, x_ref, gp_ref, gn_ref, xo_ref, xn_ref, *, scale):
    x_new = x_ref[...] + scale * _rms(h_ref[...], gp_ref[...])
    xo_ref[...] = x_new
    xn_ref[...] = _rms(x_new, gn_ref[...]).astype(xn_ref.dtype)


def _residual_last_body(h_ref, x_ref, gp_ref, xo_ref, *, scale):
    xo_ref[...] = x_ref[...] + scale * _rms(h_ref[...], gp_ref[...])


def residual_update(h, x, g_post, g_next, scale, *, tm=256):
    m, d = x.shape
    row = pl.BlockSpec((tm, d), lambda i: (i, 0))
    vec = pl.BlockSpec((1, d), lambda i: (0, 0))
    if g_next is None:
        return pl.pallas_call(
            functools.partial(_residual_last_body, scale=scale),
            grid=(m // tm,),
            in_specs=[row, row, vec],
            out_specs=row,
            out_shape=jax.ShapeDtypeStruct((m, d), F32),
            compiler_params=_params("parallel"),
            name="residual_last",
        )(h, x, g_post.reshape(1, d)), None
    return pl.pallas_call(
        functools.partial(_residual_body, scale=scale),
        grid=(m // tm,),
        in_specs=[row, row, vec, vec],
        out_specs=[row, row],
        out_shape=[jax.ShapeDtypeStruct((m, d), F32), jax.ShapeDtypeStruct((m, d), BF16)],
        compiler_params=_params("parallel"),
        name="residual_update",
    )(h, x, g_post.reshape(1, d), g_next.reshape(1, d))


def _gating_body(u_ref, v_ref, lg_ref, lb_ref, ws_ref, bs_ref, o_ref, *, n_blk, n_grp):
    v = v_ref[...].astype(F32)
    mu = jnp.mean(v, axis=-1, keepdims=True)
    vc = v - mu
    vn = vc * lax.rsqrt(jnp.mean(vc * vc, axis=-1, keepdims=True) + EPS) * lg_ref[...] + lb_ref[...]
    vn = vn.astype(BF16)
    t_pos = lax.broadcasted_iota(jnp.int32, (A_BLOCK, A_BLOCK), 0)
    s_pos = lax.broadcasted_iota(jnp.int32, (A_BLOCK, A_BLOCK), 1)
    mask = (s_pos // CHUNK) <= (t_pos // CHUNK)
    for g in range(n_grp):
        w = jnp.where(mask, ws_ref[g], 0.0).astype(BF16)
        b_col = bs_ref[:, g:g + 1]
        cols = slice(g * A_GROUP_DIM, (g + 1) * A_GROUP_DIM)
        for c in range(n_blk):
            rows = slice(c * A_BLOCK, (c + 1) * A_BLOCK)
            mixed = jnp.dot(w, vn[rows, cols], preferred_element_type=F32) + b_col
            o_ref[rows, cols] = (u_ref[rows, cols].astype(F32) * mixed).astype(o_ref.dtype)


def spatial_gating(uv, ln_g, ln_b, w_s, b_s, *, n_blk=4):
    s_len, two_w = uv.shape
    a_w = two_w // 2
    n_grp = a_w // A_GROUP_DIM
    rows = n_blk * A_BLOCK
    body = functools.partial(_gating_body, n_blk=n_blk, n_grp=n_grp)
    return pl.pallas_call(
        body,
        grid=(s_len // rows,),
        in_specs=[
            pl.BlockSpec((rows, a_w), lambda i: (i, 0)),
            pl.BlockSpec((rows, a_w), lambda i: (i, 1)),
            pl.BlockSpec((1, a_w), lambda i: (0, 0)),
            pl.BlockSpec((1, a_w), lambda i: (0, 0)),
            pl.BlockSpec((n_grp, A_BLOCK, A_BLOCK), lambda i: (0, 0, 0)),
            pl.BlockSpec((A_BLOCK, n_grp), lambda i: (0, 0)),
        ],
        out_specs=pl.BlockSpec((rows, a_w), lambda i: (i, 0)),
        out_shape=jax.ShapeDtypeStruct((s_len, a_w), BF16),
        compiler_params=_params("parallel"),
        name="spatial_gating",
    )(uv, uv, ln_g.reshape(1, a_w), ln_b.reshape(1, a_w), w_s, b_s.T)


def _softplus(z):
    return jnp.maximum(z, 0.0) + jnp.log1p(jnp.exp(-jnp.abs(z)))


def _stickbreak_body(q_ref, k_ref, v_ref, o_ref, acc_ref, *, tq, tk):
    i = pl.program_id(1)
    n_sub = tq // tk
    q = q_ref[...]
    upper = (lax.broadcasted_iota(jnp.int32, (tk, tk), 1)
             > lax.broadcasted_iota(jnp.int32, (tk, tk), 0)).astype(BF16)
    acc_ref[...] = jnp.zeros_like(acc_ref)

    def tile(j, carry, masked):
        start = pl.multiple_of(j * tk, tk)
        k_blk = k_ref[pl.ds(start, tk), :]
        v_blk = v_ref[pl.ds(start, tk), :]
        z = lax.dot_general(k_blk, q, (((1,), (1,)), ((), ())), preferred_element_type=F32)
        sp = _softplus(z)
        log_keep = -sp
        if masked:
            s_pos = j * tk + lax.broadcasted_iota(jnp.int32, (tk, tq), 0)
            t_pos = i * tq + lax.broadcasted_iota(jnp.int32, (tk, tq), 1)
            valid = s_pos < t_pos
            log_keep = jnp.where(valid, log_keep, 0.0)
        within = jnp.dot(upper, log_keep.astype(BF16), preferred_element_type=F32)
        a = jnp.exp((z - sp) + (within + carry))
        if masked:
            a = jnp.where(valid, a, 0.0)
        acc_ref[...] += lax.dot_general(v_blk, a.astype(BF16), (((0,), (0,)), ((), ())),
                                        preferred_element_type=F32)
        return carry + jnp.sum(log_keep, axis=0, keepdims=True)

    carry = jnp.zeros((1, tq), F32)
    for d in range(n_sub):
        carry = tile(i * n_sub + (n_sub - 1 - d), carry, True)

    def group(jj, carry):
        for d in range(n_sub):
            carry = tile(i * n_sub - 1 - (jj * n_sub + d), carry, False)
        return carry

    lax.fori_loop(0, i, group, carry)
    o_ref[...] = acc_ref[...].T.astype(o_ref.dtype)


def stickbreak_attention(qkv, n_heads, *, tq=512, tk=256):
    s_len = qkv.shape[0]
    dh = B_HEAD_DIM
    tq, tk = min(tq, s_len), min(tk, s_len)
    assert s_len % tq == 0 and tq % tk == 0
    body = functools.partial(_stickbreak_body, tq=tq, tk=tk)
    return pl.pallas_call(
        body,
        grid=(n_heads, s_len // tq),
        in_specs=[
            pl.BlockSpec((tq, dh), lambda h, i: (i, h)),
            pl.BlockSpec((s_len, dh), lambda h, i: (0, n_heads + h)),
            pl.BlockSpec((s_len, dh), lambda h, i: (0, 2 * n_heads + h)),
        ],
        out_specs=pl.BlockSpec((tq, dh), lambda h, i: (i, h)),
        out_shape=jax.ShapeDtypeStruct((s_len, n_heads * dh), BF16),
        scratch_shapes=[pltpu.VMEM((dh, tq), F32)],
        compiler_params=_params("parallel", "arbitrary"),
        name="stickbreak_attention",
    )(qkv, qkv, qkv)


def _conv_body(y_ref, halo_ref, w_ref, b_ref, o_ref, ext_ref, *, tt, lane_chunk):
    i = pl.program_id(0)
    width = y_ref.shape[1]
    ext_ref[0:C_HALO, :] = jnp.where(i > 0, halo_ref[...], 0.0)
    ext_ref[C_HALO:, :] = y_ref[...]
    first = C_HALO - (C_KERNEL - 1)
    for c in range(0, width, lane_chunk):
        cols = slice(c, c + lane_chunk)
        acc = jnp.zeros((tt, lane_chunk), F32) + b_ref[:, cols]
        for k in range(C_KERNEL):
            acc = acc + w_ref[k:k + 1, cols] * ext_ref[first + k:first + k + tt, cols]
        o_ref[:, cols] = acc.astype(o_ref.dtype)


def _ln_swish_body(y_ref, lg_ref, lb_ref, o_ref):
    y = y_ref[...]
    mu = jnp.mean(y, axis=-1, keepdims=True)
    yc = y - mu
    yn = yc * lax.rsqrt(jnp.mean(yc * yc, axis=-1, keepdims=True) + EPS) * lg_ref[...] + lb_ref[...]
    o_ref[...] = (yn * jax.nn.sigmoid(yn)).astype(o_ref.dtype)


def conv_ln_swish(y, w_dw, b_dw, ln_g, ln_b, *, tt=256, lane_chunk=512):
    s_len, width = y.shape
    tt = min(tt, s_len)
    lane_chunk = min(lane_chunk, width)
    halo_per_tile = tt // C_HALO
    vec = pl.BlockSpec((1, width), lambda i: (0, 0))
    conv = pl.pallas_call(
        functools.partial(_conv_body, tt=tt, lane_chunk=lane_chunk),
        grid=(s_len // tt,),
        in_specs=[
            pl.BlockSpec((tt, width), lambda i: (i, 0)),
            pl.BlockSpec((C_HALO, width), lambda i: (jnp.maximum(i * halo_per_tile - 1, 0), 0)),
            pl.BlockSpec((C_KERNEL, width), lambda i: (0, 0)),
            vec,
        ],
        out_specs=pl.BlockSpec((tt, width), lambda i: (i, 0)),
        out_shape=jax.ShapeDtypeStruct((s_len, width), F32),
        scratch_shapes=[pltpu.VMEM((tt + C_HALO, width), F32)],
        compiler_params=_params("parallel"),
        name="causal_dwconv",
    )(y, y, w_dw, b_dw.reshape(1, width))
    return pl.pallas_call(
        _ln_swish_body,
        grid=(s_len // tt,),
        in_specs=[pl.BlockSpec((tt, width), lambda i: (i, 0)), vec, vec],
        out_specs=pl.BlockSpec((tt, width), lambda i: (i, 0)),
        out_shape=jax.ShapeDtypeStruct((s_len, width), BF16),
        compiler_params=_params("parallel"),
        name="ln_swish",
    )(conv, ln_g.reshape(1, width), ln_b.reshape(1, width))


def _xattn_body(q_ref, k_ref, v_ref, o_ref, *, n_heads):
    dh = q_ref.shape[1] // n_heads
    for h in range(n_heads):
        cols = slice(h * dh, (h + 1) * dh)
        s = lax.dot_general(q_ref[:, cols], k_ref[:, cols], (((1,), (1,)), ((), ())),
                            preferred_element_type=F32)
        e = jnp.exp(s - jnp.max(s, axis=-1, keepdims=True))
        p = e / jnp.sum(e, axis=-1, keepdims=True)
        o_ref[:, cols] = jnp.dot(p.astype(BF16), v_ref[:, cols],
                                 preferred_element_type=F32).astype(o_ref.dtype)


def xattn_core(q, k, v, *, tm=512):
    s_len, width = q.shape
    n_mem = k.shape[0]
    tm = min(tm, s_len)
    return pl.pallas_call(
        functools.partial(_xattn_body, n_heads=X_HEADS),
        grid=(s_len // tm,),
        in_specs=[
            pl.BlockSpec((tm, width), lambda i: (i, 0)),
            pl.BlockSpec((n_mem, width), lambda i: (0, 0)),
            pl.BlockSpec((n_mem, width), lambda i: (0, 0)),
        ],
        out_specs=pl.BlockSpec((tm, width), lambda i: (i, 0)),
        out_shape=jax.ShapeDtypeStruct((s_len, width), BF16),
        compiler_params=_params("parallel"),
        name="xattn_core",
    )(q, k, v)


def _bf(w):
    return w.astype(BF16)


def kernel(x, mem, ffn1_pre_g, ffn1_post_g, ffn1_w_gate, ffn1_w_up, ffn1_w_down, mix_pre_g, mix_post_g, ab_w_in, ab_w_out, a_ln_g, a_ln_b, a_w_s, a_b_s, c_w_in, c_w_dw, c_b_dw, c_ln_g, c_ln_b, c_w_out, xa_pre_g, xa_post_g, xa_mem_g, xa_w_q, xa_w_k, xa_w_v, xa_w_o, ffn2_pre_g, ffn2_post_g, ffn2_w_gate, ffn2_w_up, ffn2_w_down):
    bsz, s_len, d = x.shape
    depth = ffn1_pre_g.shape[0]
    a_w = a_ln_g.shape[1]
    b_w = ab_w_out.shape[1] - a_w
    b_heads = b_w // B_HEAD_DIM
    x_w = xa_w_q.shape[2]
    q_scale = jnp.concatenate([jnp.full((1, b_w), B_HEAD_DIM ** -0.5, F32), jnp.ones((1, 2 * b_w), F32)], axis=1)
    xq_scale = jnp.full((1, x_w), (x_w // X_HEADS) ** -0.5, F32)

    def ffn(xn, w_gate, w_up, w_down):
        mid = matmul([xn], [[_bf(w_gate)], [_bf(w_up)]], _swiglu_gate, BF16, tm=1024, tn=512, name="ffn_in")
        return matmul([mid], [[_bf(w_down)]], _identity, F32, tm=1024, tn=512, name="ffn_down")

    outs = []
    for b in range(bsz):
        xb = x[b]
        mem_b = mem[b]
        xn = prenorm(xb, ffn1_pre_g[0])
        for l in range(depth):
            h = ffn(xn, ffn1_w_gate[l], ffn1_w_up[l], ffn1_w_down[l])
            xb, xn = residual_update(h, xb, ffn1_post_g[l], mix_pre_g[l], 0.5)

            if l % 2 == 0:
                e = l // 2
                w_in = ab_w_in[e]
                uv = matmul([xn], [[_bf(w_in[:, :2 * a_w])]], _gelu_tanh, BF16, tm=1024, tn=1024, name="ab_in_uv")
                qkv = matmul([xn], [[_bf(w_in[:, 2 * a_w:])]], _colscale, BF16, tm=1024, tn=1024,
                             vec=q_scale, name="ab_in_qkv")
                a_out = spatial_gating(uv, a_ln_g[e], a_ln_b[e], a_w_s[e], a_b_s[e])
                b_out = stickbreak_attention(qkv, b_heads)
                w_out = ab_w_out[e]
                h = matmul([a_out, b_out], [[_bf(w_out[:a_w]), _bf(w_out[a_w:])]], _identity, F32,
                           tm=1024, tn=1024, name="ab_out")
            else:
                o = l // 2
                w_in = c_w_in[o]
                half = w_in.shape[1] // 2
                y = matmul([xn], [[_bf(w_in[:, :half])], [_bf(w_in[:, half:])]], _glu_gate, F32,
                           tm=1024, tn=512, name="conf_in")
                y = conv_ln_swish(y, c_w_dw[o], c_b_dw[o], c_ln_g[o], c_ln_b[o])
                h = matmul([y], [[_bf(c_w_out[o])]], _identity, F32, tm=1024, tn=1024, name="conf_out")
            xb, xn = residual_update(h, xb, mix_post_g[l], xa_pre_g[l], 1.0)

            mn = prenorm(mem_b, xa_mem_g[l])
            k = matmul([mn], [[_bf(xa_w_k[l])]], _identity, BF16, tm=256, tn=1024, name="xattn_k")
            v = matmul([mn], [[_bf(xa_w_v[l])]], _identity, BF16, tm=256, tn=1024, name="xattn_v")
            q = matmul([xn], [[_bf(xa_w_q[l])]], _colscale, BF16, tm=1024, tn=1024, vec=xq_scale, name="xattn_q")
            o_att = xattn_core(q, k, v)
            h = matmul([o_att], [[_bf(xa_w_o[l])]], _identity, F32, tm=1024, tn=1024, name="xattn_o")
            xb, xn = residual_update(h, xb, xa_post_g[l], ffn2_pre_g[l], 1.0)

            h = ffn(xn, ffn2_w_gate[l], ffn2_w_up[l], ffn2_w_down[l])
            g_next = ffn1_pre_g[l + 1] if l + 1 < depth else None
            xb, xn = residual_update(h, xb, ffn2_post_g[l], g_next, 0.5)
        outs.append(xb)
    return jnp.stack(outs, axis=0)
```

```python
import functools
import math

import jax
import jax.numpy as jnp
from jax import lax
from jax.experimental import pallas as pl
from jax.experimental.pallas import tpu as pltpu

EPS = 1e-6
CHUNK = 64
A_BLOCK = 128
A_GROUP_DIM = 128
B_HEAD_DIM = 128
X_HEADS = 4
C_KERNEL = 31
C_HALO = 32

VMEM_LIMIT_BYTES = 56 * 1024 * 1024

BF16 = jnp.bfloat16
F32 = jnp.float32
LOG2E = math.log2(math.e)


def _params(*sem):
    return pltpu.CompilerParams(dimension_semantics=sem, vmem_limit_bytes=VMEM_LIMIT_BYTES)


def _mm_body(*refs, n_lhs, n_grp, epilogue, has_vec):
    o_ref = refs[-1]
    lhs = refs[:n_lhs]
    rhs = refs[n_lhs:n_lhs + n_lhs * n_grp]
    accs = []
    for g in range(n_grp):
        acc = None
        for l in range(n_lhs):
            d = jnp.dot(lhs[l][...], rhs[g * n_lhs + l][...], preferred_element_type=F32)
            acc = d if acc is None else acc + d
        accs.append(acc)
    if has_vec:
        accs.append(refs[n_lhs + n_lhs * n_grp][...])
    o_ref[...] = epilogue(*accs).astype(o_ref.dtype)


def matmul(lhs, rhs, epilogue, out_dtype, *, tm, tn, vec=None, name="matmul"):
    n_lhs, n_grp = len(lhs), len(rhs)
    m = lhs[0].shape[0]
    n = rhs[0][0].shape[1]
    tm, tn = min(tm, m), min(tn, n)
    assert m % tm == 0 and n % tn == 0
    in_specs = [pl.BlockSpec((tm, a.shape[1]), lambda i, j: (i, 0)) for a in lhs]
    flat_rhs = []
    for grp in rhs:
        assert len(grp) == n_lhs
        for a, w in zip(lhs, grp):
            assert w.shape == (a.shape[1], n)
            in_specs.append(pl.BlockSpec((w.shape[0], tn), lambda i, j: (0, j)))
            flat_rhs.append(w)
    args = list(lhs) + flat_rhs
    if vec is not None:
        in_specs.append(pl.BlockSpec((1, tn), lambda i, j: (0, j)))
        args.append(vec)
    body = functools.partial(_mm_body, n_lhs=n_lhs, n_grp=n_grp, epilogue=epilogue,
                             has_vec=vec is not None)
    return pl.pallas_call(
        body,
        grid=(m // tm, n // tn),
        in_specs=in_specs,
        out_specs=pl.BlockSpec((tm, tn), lambda i, j: (i, j)),
        out_shape=jax.ShapeDtypeStruct((m, n), out_dtype),
        compiler_params=_params("parallel", "arbitrary"),
        name=name,
    )(*args)


def _identity(a):
    return a


def _colscale(a, s):
    return a * s


def _swiglu_gate(g, u):
    return g * jax.nn.sigmoid(g) * u


def _glu_gate(a, g):
    return a * jax.nn.sigmoid(g)


def _gelu_tanh(a):
    return jax.nn.gelu(a, approximate=True)


def _rms(x, g):
    return x * lax.rsqrt(jnp.mean(x * x, axis=-1, keepdims=True) + EPS) * g


def _prenorm_body(x_ref, g_ref, o_ref):
    o_ref[...] = _rms(x_ref[...], g_ref[...]).astype(o_ref.dtype)


def prenorm(x, g, *, tm=256):
    m, d = x.shape
    tm = min(tm, m)
    return pl.pallas_call(
        _prenorm_body,
        grid=(m // tm,),
        in_specs=[pl.BlockSpec((tm, d), lambda i: (i, 0)), pl.BlockSpec((1, d), lambda i: (0, 0))],
        out_specs=pl.BlockSpec((tm, d), lambda i: (i, 0)),
        out_shape=jax.ShapeDtypeStruct((m, d), BF16),
        compiler_params=_params("parallel"),
        name="prenorm",
    )(x, g.reshape(1, d))


def _residual_body(h_ref, x_ref, gp_ref, gn_ref, xo_ref, xn_ref, *, scale):
    x_new = x_ref[...] + scale * _rms(h_ref[...], gp_ref[...])
    xo_ref[...] = x_new
    xn_ref[...] = _rms(x_new, gn_ref[...]).astype(xn_ref.dtype)


def _residual_last_body(h_ref, x_ref, gp_ref, xo_ref, *, scale):
    xo_ref[...] = x_ref[...] + scale * _rms(h_ref[...], gp_ref[...])


def residual_update(h, x, g_post, g_next, scale, *, tm=256):
    m, d = x.shape
    row = pl.BlockSpec((tm, d), lambda i: (i, 0))
    vec = pl.BlockSpec((1, d), lambda i: (0, 0))
    if g_next is None:
        return pl.pallas_call(
            functools.partial(_residual_last_body, scale=scale),
            grid=(m // tm,),
            in_specs=[row, row, vec],
            out_specs=row,
            out_shape=jax.ShapeDtypeStruct((m, d), F32),
            compiler_params=_params("parallel"),
            name="residual_last",
        )(h, x, g_post.reshape(1, d)), None
    return pl.pallas_call(
        functools.partial(_residual_body, scale=scale),
        grid=(m // tm,),
        in_specs=[row, row, vec, vec],
        out_specs=[row, row],
        out_shape=[jax.ShapeDtypeStruct((m, d), F32), jax.ShapeDtypeStruct((m, d), BF16)],
        compiler_params=_params("parallel"),
        name="residual_update",
    )(h, x, g_post.reshape(1, d), g_next.reshape(1, d))


def _gating_body(u_ref, v_ref, lg_ref, lb_ref, ws_ref, bs_ref, o_ref, *, n_blk, n_grp):
    v = v_ref[...].astype(F32)
    mu = jnp.mean(v, axis=-1, keepdims=True)
    vc = v - mu
    vn = vc * lax.rsqrt(jnp.mean(vc * vc, axis=-1, keepdims=True) + EPS) * lg_ref[...] + lb_ref[...]
    vn = vn.astype(BF16)
    t_pos = lax.broadcasted_iota(jnp.int32, (A_BLOCK, A_BLOCK), 0)
    s_pos = lax.broadcasted_iota(jnp.int32, (A_BLOCK, A_BLOCK), 1)
    mask = (s_pos // CHUNK) <= (t_pos // CHUNK)
    for g in range(n_grp):
        w = jnp.where(mask, ws_ref[g], 0.0).astype(BF16)
        b_col = bs_ref[:, g:g + 1]
        cols = slice(g * A_GROUP_DIM, (g + 1) * A_GROUP_DIM)
        for c in range(n_blk):
            rows = slice(c * A_BLOCK, (c + 1) * A_BLOCK)
            mixed = jnp.dot(w, vn[rows, cols], preferred_element_type=F32) + b_col
            o_ref[rows, cols] = (u_ref[rows, cols].astype(F32) * mixed).astype(o_ref.dtype)


def spatial_gating(uv, ln_g, ln_b, w_s, b_s, *, n_blk=4):
    s_len, two_w = uv.shape
    a_w = two_w // 2
    n_grp = a_w // A_GROUP_DIM
    rows = n_blk * A_BLOCK
    body = functools.partial(_gating_body, n_blk=n_blk, n_grp=n_grp)
    return pl.pallas_call(
        body,
        grid=(s_len // rows,),
        in_specs=[
            pl.BlockSpec((rows, a_w), lambda i: (i, 0)),
            pl.BlockSpec((rows, a_w), lambda i: (i, 1)),
            pl.BlockSpec((1, a_w), lambda i: (0, 0)),
            pl.BlockSpec((1, a_w), lambda i: (0, 0)),
            pl.BlockSpec((n_grp, A_BLOCK, A_BLOCK), lambda i: (0, 0, 0)),
            pl.BlockSpec((A_BLOCK, n_grp), lambda i: (0, 0)),
        ],
        out_specs=pl.BlockSpec((rows, a_w), lambda i: (i, 0)),
        out_shape=jax.ShapeDtypeStruct((s_len, a_w), BF16),
        compiler_params=_params("parallel"),
        name="spatial_gating",
    )(uv, uv, ln_g.reshape(1, a_w), ln_b.reshape(1, a_w), w_s, b_s.T)


def _softplus2(z):
    neg_abs = pltpu.bitcast(pltpu.bitcast(z, jnp.uint32) | jnp.uint32(0x80000000), F32)
    return jnp.maximum(z, 0.0) + jnp.log(1.0 + jnp.exp2(neg_abs)) * LOG2E


def _stickbreak_body(q_ref, k_ref, v_ref, o_ref, acc_ref, a_ref, *, tq, tk):
    i = pl.program_id(1)
    n_sub = tq // tk
    q = q_ref[...]
    neg_upper = jnp.where(lax.broadcasted_iota(jnp.int32, (tk, tk), 1)
                          > lax.broadcasted_iota(jnp.int32, (tk, tk), 0), -1.0, 0.0).astype(BF16)
    acc_ref[...] = jnp.zeros_like(acc_ref)

    def accumulate(js):
        for d, j in enumerate(js):
            v_blk = v_ref[pl.ds(pl.multiple_of(j * tk, tk), tk), :]
            acc_ref[...] += lax.dot_general(v_blk, a_ref[d], (((0,), (0,)), ((), ())),
                                            preferred_element_type=F32)

    def tiles(js, prev_js, carry, masked):
        zs = []
        for j in js:
            k_blk = k_ref[pl.ds(pl.multiple_of(j * tk, tk), tk), :]
            zs.append(lax.dot_general(k_blk, q, (((1,), (1,)), ((), ())), preferred_element_type=F32))
        if prev_js is not None:
            accumulate(prev_js)
        sps, zcs = [], []
        for j, z in zip(js, zs):
            sp = _softplus2(z)
            zc = (z - sp) + carry
            if masked:
                s_pos = j * tk + lax.broadcasted_iota(jnp.int32, (tk, tq), 0)
                t_pos = i * tq + lax.broadcasted_iota(jnp.int32, (tk, tq), 1)
                valid = s_pos < t_pos
                sp = jnp.where(valid, sp, 0.0)
                zc = jnp.where(valid, zc, -1e30)
            carry = carry - jnp.sum(sp, axis=0, keepdims=True)
            sps.append(sp.astype(BF16))
            zcs.append(zc)
        withins = [jnp.dot(neg_upper, sp, preferred_element_type=F32) for sp in sps]
        for d, (zc, w) in enumerate(zip(zcs, withins)):
            a_ref[d] = jnp.exp2(zc + w).astype(BF16)
        return carry

    def group_tiles(g):
        return [g * n_sub + (n_sub - 1 - d) for d in range(n_sub)]

    carry = tiles(group_tiles(i), None, jnp.zeros((1, tq), F32), True)

    def group(jj, carry):
        return tiles(group_tiles(i - 1 - jj), group_tiles(i - jj), carry, False)

    lax.fori_loop(0, i, group, carry)
    accumulate(group_tiles(0))
    o_ref[...] = acc_ref[...].T.astype(o_ref.dtype)


def stickbreak_attention(qkv, n_heads, *, tq=1024, tk=256):
    s_len = qkv.shape[0]
    dh = B_HEAD_DIM
    tq, tk = min(tq, s_len), min(tk, s_len)
    assert s_len % tq == 0 and tq % tk == 0
    body = functools.partial(_stickbreak_body, tq=tq, tk=tk)
    return pl.pallas_call(
        body,
        grid=(n_heads, s_len // tq),
        in_specs=[
            pl.BlockSpec((tq, dh), lambda h, i: (i, h)),
            pl.BlockSpec((s_len, dh), lambda h, i: (0, n_heads + h)),
            pl.BlockSpec((s_len, dh), lambda h, i: (0, 2 * n_heads + h)),
        ],
        out_specs=pl.BlockSpec((tq, dh), lambda h, i: (i, h)),
        out_shape=jax.ShapeDtypeStruct((s_len, n_heads * dh), BF16),
        scratch_shapes=[pltpu.VMEM((dh, tq), F32), pltpu.VMEM((tq // tk, tk, tq), BF16)],
        compiler_params=_params("parallel", "arbitrary"),
        name="stickbreak_attention",
    )(qkv, qkv, qkv)


SUBLANES = 8
LANES = 128


def _conv_body(y_ref, halo_ref, w_ref, b_ref, lg_ref, lb_ref, o_ref, ext_ref, conv_ref, *, tt, row_chunk):
    i = pl.program_id(0)
    width = y_ref.shape[1]
    ext_ref[0:C_HALO, :] = jnp.where(i > 0, halo_ref[...], 0.0)
    ext_ref[C_HALO:, :] = y_ref[...]
    first = C_HALO - (C_KERNEL - 1)
    last = first + C_KERNEL - 1

    def lane_chunk(c, _):
        cols = pl.ds(pl.multiple_of(c * LANES, LANES), LANES)
        bias = b_ref[:, cols]
        for r0 in range(0, tt, row_chunk):
            acc = jnp.zeros((row_chunk, LANES), F32) + bias
            for phase in range(SUBLANES):
                steps = [a for a in range(last // SUBLANES + 1) if first <= a * SUBLANES + phase <= last]
                span = row_chunk + steps[-1] * SUBLANES
                if phase:
                    window = ext_ref[pl.ds(r0, span + SUBLANES), cols]
                    shifted = pltpu.roll(window, span + SUBLANES - phase, axis=0)
                else:
                    shifted = ext_ref[pl.ds(r0, span), cols]
                for a in steps:
                    k = a * SUBLANES + phase - first
                    acc = acc + w_ref[k:k + 1, cols] * shifted[a * SUBLANES:a * SUBLANES + row_chunk]
            conv_ref[r0:r0 + row_chunk, cols] = acc
        return 0

    lax.fori_loop(0, width // LANES, lane_chunk, 0)
    y = conv_ref[...]
    mu = jnp.mean(y, axis=-1, keepdims=True)
    yc = y - mu
    yn = yc * lax.rsqrt(jnp.mean(yc * yc, axis=-1, keepdims=True) + EPS) * lg_ref[...] + lb_ref[...]
    o_ref[...] = (yn * jax.nn.sigmoid(yn)).astype(o_ref.dtype)


def conv_ln_swish(y, w_dw, b_dw, ln_g, ln_b, *, tt=256, row_chunk=128):
    s_len, width = y.shape
    tt = min(tt, s_len)
    halo_per_tile = tt // C_HALO
    vec = pl.BlockSpec((1, width), lambda i: (0, 0))
    return pl.pallas_call(
        functools.partial(_conv_body, tt=tt, row_chunk=row_chunk),
        grid=(s_len // tt,),
        in_specs=[
            pl.BlockSpec((tt, width), lambda i: (i, 0)),
            pl.BlockSpec((C_HALO, width), lambda i: (jnp.maximum(i * halo_per_tile - 1, 0), 0)),
            pl.BlockSpec((C_KERNEL, width), lambda i: (0, 0)),
            vec, vec, vec,
        ],
        out_specs=pl.BlockSpec((tt, width), lambda i: (i, 0)),
        out_shape=jax.ShapeDtypeStruct((s_len, width), BF16),
        scratch_shapes=[pltpu.VMEM((tt + C_HALO, width), F32), pltpu.VMEM((tt, width), F32)],
        compiler_params=_params("parallel"),
        name="conv_ln_swish",
    )(y, y, w_dw, b_dw.reshape(1, width), ln_g.reshape(1, width), ln_b.reshape(1, width))


def _xattn_body(q_ref, k_ref, v_ref, o_ref, *, n_heads):
    dh = q_ref.shape[1] // n_heads
    for h in range(n_heads):
        cols = slice(h * dh, (h + 1) * dh)
        s = lax.dot_general(q_ref[:, cols], k_ref[:, cols], (((1,), (1,)), ((), ())),
                            preferred_element_type=F32)
        e = jnp.exp(s - jnp.max(s, axis=-1, keepdims=True))
        p = e / jnp.sum(e, axis=-1, keepdims=True)
        o_ref[:, cols] = jnp.dot(p.astype(BF16), v_ref[:, cols],
                                 preferred_element_type=F32).astype(o_ref.dtype)


def xattn_core(q, k, v, *, tm=512):
    s_len, width = q.shape
    n_mem = k.shape[0]
    tm = min(tm, s_len)
    return pl.pallas_call(
        functools.partial(_xattn_body, n_heads=X_HEADS),
        grid=(s_len // tm,),
        in_specs=[
            pl.BlockSpec((tm, width), lambda i: (i, 0)),
            pl.BlockSpec((n_mem, width), lambda i: (0, 0)),
            pl.BlockSpec((n_mem, width), lambda i: (0, 0)),
        ],
        out_specs=pl.BlockSpec((tm, width), lambda i: (i, 0)),
        out_shape=jax.ShapeDtypeStruct((s_len, width), BF16),
        compiler_params=_params("parallel"),
        name="xattn_core",
    )(q, k, v)


def _bf(w):
    return w.astype(BF16)


def kernel(x, mem, ffn1_pre_g, ffn1_post_g, ffn1_w_gate, ffn1_w_up, ffn1_w_down, mix_pre_g, mix_post_g, ab_w_in, ab_w_out, a_ln_g, a_ln_b, a_w_s, a_b_s, c_w_in, c_w_dw, c_b_dw, c_ln_g, c_ln_b, c_w_out, xa_pre_g, xa_post_g, xa_mem_g, xa_w_q, xa_w_k, xa_w_v, xa_w_o, ffn2_pre_g, ffn2_post_g, ffn2_w_gate, ffn2_w_up, ffn2_w_down):
    bsz, s_len, d = x.shape
    depth = ffn1_pre_g.shape[0]
    a_w = a_ln_g.shape[1]
    b_w = ab_w_out.shape[1] - a_w
    b_heads = b_w // B_HEAD_DIM
    x_w = xa_w_q.shape[2]
    q_scale = jnp.concatenate([jnp.full((1, b_w), B_HEAD_DIM ** -0.5 * LOG2E, F32), jnp.ones((1, 2 * b_w), F32)], axis=1)
    xq_scale = jnp.full((1, x_w), (x_w // X_HEADS) ** -0.5, F32)

    def ffn(xn, w_gate, w_up, w_down):
        mid = matmul([xn], [[_bf(w_gate)], [_bf(w_up)]], _swiglu_gate, BF16, tm=1024, tn=512, name="ffn_in")
        return matmul([mid], [[_bf(w_down)]], _identity, F32, tm=1024, tn=512, name="ffn_down")

    outs = []
    for b in range(bsz):
        xb = x[b]
        mem_b = mem[b]
        xn = prenorm(xb, ffn1_pre_g[0])
        for l in range(depth):
            h = ffn(xn, ffn1_w_gate[l], ffn1_w_up[l], ffn1_w_down[l])
            xb, xn = residual_update(h, xb, ffn1_post_g[l], mix_pre_g[l], 0.5)

            if l % 2 == 0:
                e = l // 2
                w_in = ab_w_in[e]
                uv = matmul([xn], [[_bf(w_in[:, :2 * a_w])]], _gelu_tanh, BF16, tm=1024, tn=1024, name="ab_in_uv")
                qkv = matmul([xn], [[_bf(w_in[:, 2 * a_w:])]], _colscale, BF16, tm=1024, tn=1024,
                             vec=q_scale, name="ab_in_qkv")
                a_out = spatial_gating(uv, a_ln_g[e], a_ln_b[e], a_w_s[e], a_b_s[e])
                b_out = stickbreak_attention(qkv, b_heads)
                w_out = ab_w_out[e]
                h = matmul([a_out, b_out], [[_bf(w_out[:a_w]), _bf(w_out[a_w:])]], _identity, F32,
                           tm=1024, tn=1024, name="ab_out")
            else:
                o = l // 2
                w_in = c_w_in[o]
                half = w_in.shape[1] // 2
                y = matmul([xn], [[_bf(w_in[:, :half])], [_bf(w_in[:, half:])]], _glu_gate, F32,
                           tm=1024, tn=512, name="conf_in")
                y = conv_ln_swish(y, c_w_dw[o], c_b_dw[o], c_ln_g[o], c_ln_b[o])
                h = matmul([y], [[_bf(c_w_out[o])]], _identity, F32, tm=1024, tn=1024, name="conf_out")
            xb, xn = residual_update(h, xb, mix_post_g[l], xa_pre_g[l], 1.0)

            mn = prenorm(mem_b, xa_mem_g[l])
            k = matmul([mn], [[_bf(xa_w_k[l])]], _identity, BF16, tm=256, tn=1024, name="xattn_k")
            v = matmul([mn], [[_bf(xa_w_v[l])]], _identity, BF16, tm=256, tn=1024, name="xattn_v")
            q = matmul([xn], [[_bf(xa_w_q[l])]], _colscale, BF16, tm=1024, tn=1024, vec=xq_scale, name="xattn_q")
            o_att = xattn_core(q, k, v)
            h = matmul([o_att], [[_bf(xa_w_o[l])]], _identity, F32, tm=1024, tn=1024, name="xattn_o")
            xb, xn = residual_update(h, xb, xa_post_g[l], ffn2_pre_g[l], 1.0)

            h = ffn(xn, ffn2_w_gate[l], ffn2_w_up[l], ffn2_w_down[l])
            g_next = ffn1_pre_g[l + 1] if l + 1 < depth else None
            xb, xn = residual_update(h, xb, ffn2_post_g[l], g_next, 0.5)
        outs.append(xb)
    return jnp.stack(outs, axis=0)
```

```python
import functools
import math
from typing import NamedTuple

import jax
import jax.numpy as jnp
from jax import lax
from jax.experimental import pallas as pl
from jax.experimental.pallas import tpu as pltpu

EPS = 1e-6
CHUNK = 64
A_BLOCK = 128
A_GROUP_DIM = 128
B_HEAD_DIM = 128
X_HEADS = 4
C_KERNEL = 31
C_HALO = 32

VMEM_LIMIT_BYTES = 56 * 1024 * 1024

BF16 = jnp.bfloat16
F32 = jnp.float32
LOG2E = math.log2(math.e)


def _params(*sem):
    return pltpu.CompilerParams(dimension_semantics=sem, vmem_limit_bytes=VMEM_LIMIT_BYTES)


def _mm_body(*refs, n_lhs, n_grp, epilogue, has_vec):
    o_ref = refs[-1]
    lhs = refs[:n_lhs]
    rhs = refs[n_lhs:n_lhs + n_lhs * n_grp]
    accs = []
    for g in range(n_grp):
        acc = None
        for l in range(n_lhs):
            d = jnp.dot(lhs[l][...], rhs[g * n_lhs + l][...], preferred_element_type=F32)
            acc = d if acc is None else acc + d
        accs.append(acc)
    if has_vec:
        accs.append(refs[n_lhs + n_lhs * n_grp][...])
    o_ref[...] = epilogue(*accs).astype(o_ref.dtype)


class W(NamedTuple):
    arr: jax.Array
    layer: int = 0
    row0: int = 0
    col0: int = 0

    def spec(self, k, tn, col_block):
        assert self.row0 % k == 0 and self.col0 % tn == 0
        layer, rb, cb = self.layer, self.row0 // k, self.col0 // tn
        return pl.BlockSpec((None, k, tn), lambda i, j: (layer, rb, cb + col_block(i, j)))


def _as_window(w):
    return w if isinstance(w, W) else W(w[None])


def matmul(lhs, rhs, epilogue, out_dtype, *, n, tm, tn, vec=None, name="matmul"):
    n_lhs, n_grp = len(lhs), len(rhs)
    m = lhs[0].shape[0]
    tm, tn = min(tm, m), min(tn, n)
    assert m % tm == 0 and n % tn == 0
    in_specs = [pl.BlockSpec((tm, a.shape[1]), lambda i, j: (i, 0)) for a in lhs]
    flat_rhs = []
    for grp in rhs:
        assert len(grp) == n_lhs
        for a, w in zip(lhs, grp):
            w = _as_window(w)
            in_specs.append(w.spec(a.shape[1], tn, lambda i, j: j))
            flat_rhs.append(w.arr)
    args = list(lhs) + flat_rhs
    if vec is not None:
        in_specs.append(pl.BlockSpec((1, tn), lambda i, j: (0, j)))
        args.append(vec)
    body = functools.partial(_mm_body, n_lhs=n_lhs, n_grp=n_grp, epilogue=epilogue,
                             has_vec=vec is not None)
    return pl.pallas_call(
        body,
        grid=(m // tm, n // tn),
        in_specs=in_specs,
        out_specs=pl.BlockSpec((tm, tn), lambda i, j: (i, j)),
        out_shape=jax.ShapeDtypeStruct((m, n), out_dtype),
        compiler_params=_params("parallel", "arbitrary"),
        name=name,
    )(*args)


def _mm_residual_body(*refs, n_lhs, n_col, tn, scale, has_next):
    lhs, rhs = refs[:n_lhs], refs[n_lhs:2 * n_lhs]
    rest = list(refs[2 * n_lhs:])
    gp_ref = rest.pop(0)
    gn_ref = rest.pop(0) if has_next else None
    x_hbm, xo_hbm = rest.pop(0), rest.pop(0)
    xn_hbm = rest.pop(0) if has_next else None
    h_scr, x_buf = rest.pop(0), rest.pop(0)
    xn_buf = rest.pop(0) if has_next else None
    sems = rest.pop(0)
    i, j = pl.program_id(0), pl.program_id(1)
    tm = x_buf.shape[0]

    def rows(r):
        return pl.ds(pl.multiple_of(r * tm, tm), tm)

    def x_in(r):
        return pltpu.make_async_copy(x_hbm.at[rows(r)], x_buf, sems.at[0])

    def outs(r):
        copies = [pltpu.make_async_copy(x_buf, xo_hbm.at[rows(r)], sems.at[1])]
        if has_next:
            copies.append(pltpu.make_async_copy(xn_buf, xn_hbm.at[rows(r)], sems.at[2]))
        return copies

    @pl.when(j < n_col)
    def _():
        acc = None
        for l in range(n_lhs):
            d = jnp.dot(lhs[l][...], rhs[l][...], preferred_element_type=F32)
            acc = d if acc is None else acc + d
        h_scr[:, pl.ds(pl.multiple_of(j * tn, tn), tn)] = acc

    @pl.when(j == 1)
    def _():
        @pl.when(i > 0)
        def _():
            for c in outs(i - 1):
                c.wait()
        x_in(i).start()

    @pl.when(j == n_col)
    def _():
        x_in(i).wait()
        x_new = x_buf[...] + _rms(h_scr[...], scale * gp_ref[...])
        x_buf[...] = x_new
        if has_next:
            xn_buf[...] = _rms(x_new, gn_ref[...]).astype(xn_buf.dtype)
        for c in outs(i):
            c.start()

        @pl.when(i == pl.num_programs(0) - 1)
        def _():
            for c in outs(i):
                c.wait()


def matmul_residual(lhs, rhs, x, g_post, g_next, scale, *, tm, tn, name):
    n_lhs = len(lhs)
    m, d = x.shape
    tm, tn = min(tm, m), min(tn, d)
    n_col = d // tn
    assert m % tm == 0 and d % tn == 0 and n_col >= 2
    assert math.frexp(scale)[0] == 0.5, "scale is folded into the gain row, exact only for powers of two"
    has_next = g_next is not None
    in_specs = [pl.BlockSpec((tm, a.shape[1]), lambda i, j: (i, 0)) for a in lhs]
    args = list(lhs)
    for a, w in zip(lhs, rhs):
        w = _as_window(w)
        in_specs.append(w.spec(a.shape[1], tn, lambda i, j: jnp.minimum(j, n_col - 1)))
        args.append(w.arr)
    vec = pl.BlockSpec((1, d), lambda i, j: (0, 0))
    hbm = pl.BlockSpec(memory_space=pl.ANY)
    in_specs += [vec] + ([vec] if has_next else []) + [hbm]
    args += [g_post.reshape(1, d)] + ([g_next.reshape(1, d)] if has_next else []) + [x]
    out_shape = [jax.ShapeDtypeStruct((m, d), F32)] + ([jax.ShapeDtypeStruct((m, d), BF16)] if has_next else [])
    scratch = [pltpu.VMEM((tm, d), F32), pltpu.VMEM((tm, d), F32)]
    scratch += [pltpu.VMEM((tm, d), BF16)] if has_next else []
    scratch += [pltpu.SemaphoreType.DMA((3,))]
    body = functools.partial(_mm_residual_body, n_lhs=n_lhs, n_col=n_col, tn=tn, scale=scale, has_next=has_next)
    out = pl.pallas_call(
        body,
        grid=(m // tm, n_col + 1),
        in_specs=in_specs,
        out_specs=[hbm] * len(out_shape),
        out_shape=out_shape,
        scratch_shapes=scratch,
        compiler_params=_params("arbitrary", "arbitrary"),
        name=name,
    )(*args)
    return (out[0], out[1]) if has_next else (out[0], None)


def _identity(a):
    return a


def _colscale(a, s):
    return a * s


def _swiglu_gate(g, u):
    return g * jax.nn.sigmoid(g) * u


def _glu_gate(a, g):
    return a * jax.nn.sigmoid(g)


def _gelu_tanh(a):
    return jax.nn.gelu(a, approximate=True)


def _rms(x, g):
    return x * lax.rsqrt(jnp.mean(x * x, axis=-1, keepdims=True) + EPS) * g


def _prenorm_body(x_ref, g_ref, o_ref):
    o_ref[...] = _rms(x_ref[...], g_ref[...]).astype(o_ref.dtype)


def prenorm(x, g, *, tm=256):
    m, d = x.shape
    tm = min(tm, m)
    return pl.pallas_call(
        _prenorm_body,
        grid=(m // tm,),
        in_specs=[pl.BlockSpec((tm, d), lambda i: (i, 0)), pl.BlockSpec((1, d), lambda i: (0, 0))],
        out_specs=pl.BlockSpec((tm, d), lambda i: (i, 0)),
        out_shape=jax.ShapeDtypeStruct((m, d), BF16),
        compiler_params=_params("parallel"),
        name="prenorm",
    )(x, g.reshape(1, d))


def _gating_body(u_ref, v_ref, lg_ref, lb_ref, ws_ref, bs_ref, o_ref, *, n_blk, n_grp):
    v = v_ref[...].astype(F32)
    mu = jnp.mean(v, axis=-1, keepdims=True)
    vc = v - mu
    vn = vc * lax.rsqrt(jnp.mean(vc * vc, axis=-1, keepdims=True) + EPS) * lg_ref[...] + lb_ref[...]
    vn = vn.astype(BF16)
    t_pos = lax.broadcasted_iota(jnp.int32, (A_BLOCK, A_BLOCK), 0)
    s_pos = lax.broadcasted_iota(jnp.int32, (A_BLOCK, A_BLOCK), 1)
    mask = (s_pos // CHUNK) <= (t_pos // CHUNK)
    for g in range(n_grp):
        w = jnp.where(mask, ws_ref[g], 0.0).astype(BF16)
        b_col = bs_ref[:, g:g + 1]
        cols = slice(g * A_GROUP_DIM, (g + 1) * A_GROUP_DIM)
        for c in range(n_blk):
            rows = slice(c * A_BLOCK, (c + 1) * A_BLOCK)
            mixed = jnp.dot(w, vn[rows, cols], preferred_element_type=F32) + b_col
            o_ref[rows, cols] = (u_ref[rows, cols].astype(F32) * mixed).astype(o_ref.dtype)


def spatial_gating(uv, ln_g, ln_b, w_s, b_s, *, n_blk=4):
    s_len, two_w = uv.shape
    a_w = two_w // 2
    n_grp = a_w // A_GROUP_DIM
    rows = n_blk * A_BLOCK
    body = functools.partial(_gating_body, n_blk=n_blk, n_grp=n_grp)
    return pl.pallas_call(
        body,
        grid=(s_len // rows,),
        in_specs=[
            pl.BlockSpec((rows, a_w), lambda i: (i, 0)),
            pl.BlockSpec((rows, a_w), lambda i: (i, 1)),
            pl.BlockSpec((1, a_w), lambda i: (0, 0)),
            pl.BlockSpec((1, a_w), lambda i: (0, 0)),
            pl.BlockSpec((n_grp, A_BLOCK, A_BLOCK), lambda i: (0, 0, 0)),
            pl.BlockSpec((A_BLOCK, n_grp), lambda i: (0, 0)),
        ],
        out_specs=pl.BlockSpec((rows, a_w), lambda i: (i, 0)),
        out_shape=jax.ShapeDtypeStruct((s_len, a_w), BF16),
        compiler_params=_params("parallel"),
        name="spatial_gating",
    )(uv, uv, ln_g.reshape(1, a_w), ln_b.reshape(1, a_w), w_s, b_s.T)


def _softplus2(z):
    neg_abs = pltpu.bitcast(pltpu.bitcast(z, jnp.uint32) | jnp.uint32(0x80000000), F32)
    return jnp.maximum(z, 0.0) + jnp.log(1.0 + jnp.exp2(neg_abs)) * LOG2E


def _stickbreak_body(q_ref, k_ref, v_ref, o_ref, acc_ref, a_ref, *, tq, tk):
    i = pl.program_id(1)
    n_sub = tq // tk
    q = q_ref[...]
    neg_upper = jnp.where(lax.broadcasted_iota(jnp.int32, (tk, tk), 1)
                          > lax.broadcasted_iota(jnp.int32, (tk, tk), 0), -1.0, 0.0).astype(BF16)
    acc_ref[...] = jnp.zeros_like(acc_ref)

    def accumulate(js):
        for d, j in enumerate(js):
            v_blk = v_ref[pl.ds(pl.multiple_of(j * tk, tk), tk), :]
            acc_ref[...] += lax.dot_general(v_blk, a_ref[d], (((0,), (0,)), ((), ())),
                                            preferred_element_type=F32)

    def tiles(js, prev_js, carry, masked):
        zs = []
        for j in js:
            k_blk = k_ref[pl.ds(pl.multiple_of(j * tk, tk), tk), :]
            zs.append(lax.dot_general(k_blk, q, (((1,), (1,)), ((), ())), preferred_element_type=F32))
        if prev_js is not None:
            accumulate(prev_js)
        sps, zcs = [], []
        for j, z in zip(js, zs):
            sp = _softplus2(z)
            zc = (z - sp) + carry
            if masked:
                s_pos = j * tk + lax.broadcasted_iota(jnp.int32, (tk, tq), 0)
                t_pos = i * tq + lax.broadcasted_iota(jnp.int32, (tk, tq), 1)
                valid = s_pos < t_pos
                sp = jnp.where(valid, sp, 0.0)
                zc = jnp.where(valid, zc, -1e30)
            carry = carry - jnp.sum(sp, axis=0, keepdims=True)
            sps.append(sp.astype(BF16))
            zcs.append(zc)
        withins = [jnp.dot(neg_upper, sp, preferred_element_type=F32) for sp in sps]
        for d, (zc, w) in enumerate(zip(zcs, withins)):
            a_ref[d] = jnp.exp2(zc + w).astype(BF16)
        return carry

    def group_tiles(g):
        return [g * n_sub + (n_sub - 1 - d) for d in range(n_sub)]

    carry = tiles(group_tiles(i), None, jnp.zeros((1, tq), F32), True)

    def group(jj, carry):
        return tiles(group_tiles(i - 1 - jj), group_tiles(i - jj), carry, False)

    lax.fori_loop(0, i, group, carry)
    accumulate(group_tiles(0))
    o_ref[...] = acc_ref[...].T.astype(o_ref.dtype)


def stickbreak_attention(qkv, n_heads, *, tq=1024, tk=256):
    s_len = qkv.shape[0]
    dh = B_HEAD_DIM
    tq, tk = min(tq, s_len), min(tk, s_len)
    assert s_len % tq == 0 and tq % tk == 0
    body = functools.partial(_stickbreak_body, tq=tq, tk=tk)
    return pl.pallas_call(
        body,
        grid=(n_heads, s_len // tq),
        in_specs=[
            pl.BlockSpec((tq, dh), lambda h, i: (i, h)),
            pl.BlockSpec((s_len, dh), lambda h, i: (0, n_heads + h)),
            pl.BlockSpec((s_len, dh), lambda h, i: (0, 2 * n_heads + h)),
        ],
        out_specs=pl.BlockSpec((tq, dh), lambda h, i: (i, h)),
        out_shape=jax.ShapeDtypeStruct((s_len, n_heads * dh), BF16),
        scratch_shapes=[pltpu.VMEM((dh, tq), F32), pltpu.VMEM((tq // tk, tk, tq), BF16)],
        compiler_params=_params("parallel", "arbitrary"),
        name="stickbreak_attention",
    )(qkv, qkv, qkv)


SUBLANES = 8
LANES = 128


def _conv_body(y_ref, halo_ref, w_ref, b_ref, lg_ref, lb_ref, o_ref, ext_ref, conv_ref, *, tt, row_chunk):
    i = pl.program_id(0)
    width = y_ref.shape[1]
    ext_ref[0:C_HALO, :] = jnp.where(i > 0, halo_ref[...], 0.0)
    ext_ref[C_HALO:, :] = y_ref[...]
    first = C_HALO - (C_KERNEL - 1)
    last = first + C_KERNEL - 1

    def lane_chunk(c, _):
        cols = pl.ds(pl.multiple_of(c * LANES, LANES), LANES)
        bias = b_ref[:, cols]
        for r0 in range(0, tt, row_chunk):
            acc = jnp.zeros((row_chunk, LANES), F32) + bias
            for phase in range(SUBLANES):
                steps = [a for a in range(last // SUBLANES + 1) if first <= a * SUBLANES + phase <= last]
                span = row_chunk + steps[-1] * SUBLANES
                if phase:
                    window = ext_ref[pl.ds(r0, span + SUBLANES), cols]
                    shifted = pltpu.roll(window, span + SUBLANES - phase, axis=0)
                else:
                    shifted = ext_ref[pl.ds(r0, span), cols]
                for a in steps:
                    k = a * SUBLANES + phase - first
                    acc = acc + w_ref[k:k + 1, cols] * shifted[a * SUBLANES:a * SUBLANES + row_chunk]
            conv_ref[r0:r0 + row_chunk, cols] = acc
        return 0

    lax.fori_loop(0, width // LANES, lane_chunk, 0)
    y = conv_ref[...]
    mu = jnp.mean(y, axis=-1, keepdims=True)
    yc = y - mu
    yn = yc * lax.rsqrt(jnp.mean(yc * yc, axis=-1, keepdims=True) + EPS) * lg_ref[...] + lb_ref[...]
    o_ref[...] = (yn * jax.nn.sigmoid(yn)).astype(o_ref.dtype)


def conv_ln_swish(y, w_dw, b_dw, ln_g, ln_b, *, tt=256, row_chunk=128):
    s_len, width = y.shape
    tt = min(tt, s_len)
    halo_per_tile = tt // C_HALO
    vec = pl.BlockSpec((1, width), lambda i: (0, 0))
    return pl.pallas_call(
        functools.partial(_conv_body, tt=tt, row_chunk=row_chunk),
        grid=(s_len // tt,),
        in_specs=[
            pl.BlockSpec((tt, width), lambda i: (i, 0)),
            pl.BlockSpec((C_HALO, width), lambda i: (jnp.maximum(i * halo_per_tile - 1, 0), 0)),
            pl.BlockSpec((C_KERNEL, width), lambda i: (0, 0)),
            vec, vec, vec,
        ],
        out_specs=pl.BlockSpec((tt, width), lambda i: (i, 0)),
        out_shape=jax.ShapeDtypeStruct((s_len, width), BF16),
        scratch_shapes=[pltpu.VMEM((tt + C_HALO, width), F32), pltpu.VMEM((tt, width), F32)],
        compiler_params=_params("parallel"),
        name="conv_ln_swish",
    )(y, y, w_dw, b_dw.reshape(1, width), ln_g.reshape(1, width), ln_b.reshape(1, width))


def _xattn_body(q_ref, k_ref, v_ref, o_ref, *, n_heads):
    dh = q_ref.shape[1] // n_heads
    for h in range(n_heads):
        cols = slice(h * dh, (h + 1) * dh)
        s = lax.dot_general(q_ref[:, cols], k_ref[:, cols], (((1,), (1,)), ((), ())),
                            preferred_element_type=F32)
        e = jnp.exp(s - jnp.max(s, axis=-1, keepdims=True))
        p = e / jnp.sum(e, axis=-1, keepdims=True)
        o_ref[:, cols] = jnp.dot(p.astype(BF16), v_ref[:, cols],
                                 preferred_element_type=F32).astype(o_ref.dtype)


def xattn_core(q, k, v, *, tm=512):
    s_len, width = q.shape
    n_mem = k.shape[0]
    tm = min(tm, s_len)
    return pl.pallas_call(
        functools.partial(_xattn_body, n_heads=X_HEADS),
        grid=(s_len // tm,),
        in_specs=[
            pl.BlockSpec((tm, width), lambda i: (i, 0)),
            pl.BlockSpec((n_mem, width), lambda i: (0, 0)),
            pl.BlockSpec((n_mem, width), lambda i: (0, 0)),
        ],
        out_specs=pl.BlockSpec((tm, width), lambda i: (i, 0)),
        out_shape=jax.ShapeDtypeStruct((s_len, width), BF16),
        compiler_params=_params("parallel"),
        name="xattn_core",
    )(q, k, v)


TM = 1024
TM_TAIL = 512
TN = 1024
TN_PAIR = 512
TN_DOWN = 512


def kernel(x, mem, ffn1_pre_g, ffn1_post_g, ffn1_w_gate, ffn1_w_up, ffn1_w_down, mix_pre_g, mix_post_g, ab_w_in, ab_w_out, a_ln_g, a_ln_b, a_w_s, a_b_s, c_w_in, c_w_dw, c_b_dw, c_ln_g, c_ln_b, c_w_out, xa_pre_g, xa_post_g, xa_mem_g, xa_w_q, xa_w_k, xa_w_v, xa_w_o, ffn2_pre_g, ffn2_post_g, ffn2_w_gate, ffn2_w_up, ffn2_w_down):
    bsz, s_len, d = x.shape
    depth = ffn1_pre_g.shape[0]
    d_ff = ffn1_w_gate.shape[2]
    a_w = a_ln_g.shape[1]
    b_w = ab_w_out.shape[1] - a_w
    b_heads = b_w // B_HEAD_DIM
    c_w = c_w_out.shape[1]
    x_w = xa_w_q.shape[2]
    q_scale = jnp.concatenate([jnp.full((1, b_w), B_HEAD_DIM ** -0.5 * LOG2E, F32), jnp.ones((1, 2 * b_w), F32)], axis=1)
    xq_scale = jnp.full((1, x_w), (x_w // X_HEADS) ** -0.5, F32)

    bf = lambda w: w.astype(BF16)
    ffn_w = [(bf(ffn1_w_gate), bf(ffn1_w_up), bf(ffn1_w_down)), (bf(ffn2_w_gate), bf(ffn2_w_up), bf(ffn2_w_down))]
    ab_w_in, ab_w_out, c_w_in, c_w_out = bf(ab_w_in), bf(ab_w_out), bf(c_w_in), bf(c_w_out)
    xa_w_q, xa_w_k, xa_w_v, xa_w_o = bf(xa_w_q), bf(xa_w_k), bf(xa_w_v), bf(xa_w_o)

    def ffn(which, l, xn, xb, g_post, g_next):
        w_gate, w_up, w_down = ffn_w[which]
        mid = matmul([xn], [[W(w_gate, l)], [W(w_up, l)]], _swiglu_gate, BF16, n=d_ff, tm=TM, tn=TN_PAIR, name="ffn_in")
        return matmul_residual([mid], [W(w_down, l)], xb, g_post, g_next, 0.5, tm=TM_TAIL, tn=TN_DOWN, name="ffn_down")

    outs = []
    for b in range(bsz):
        xb = x[b]
        mem_b = mem[b]
        xn = prenorm(xb, ffn1_pre_g[0])
        for l in range(depth):
            xb, xn = ffn(0, l, xn, xb, ffn1_post_g[l], mix_pre_g[l])

            if l % 2 == 0:
                e = l // 2
                uv = matmul([xn], [[W(ab_w_in, e)]], _gelu_tanh, BF16, n=2 * a_w, tm=TM, tn=TN, name="ab_in_uv")
                qkv = matmul([xn], [[W(ab_w_in, e, col0=2 * a_w)]], _colscale, BF16, n=3 * b_w, tm=TM, tn=TN,
                             vec=q_scale, name="ab_in_qkv")
                a_out = spatial_gating(uv, a_ln_g[e], a_ln_b[e], a_w_s[e], a_b_s[e])
                b_out = stickbreak_attention(qkv, b_heads)
                xb, xn = matmul_residual([a_out, b_out], [W(ab_w_out, e), W(ab_w_out, e, row0=a_w)], xb,
                                         mix_post_g[l], xa_pre_g[l], 1.0, tm=TM_TAIL, tn=TN, name="ab_out")
            else:
                o = l // 2
                y = matmul([xn], [[W(c_w_in, o)], [W(c_w_in, o, col0=c_w)]], _glu_gate, F32, n=c_w,
                           tm=TM, tn=TN_PAIR, name="conf_in")
                y = conv_ln_swish(y, c_w_dw[o], c_b_dw[o], c_ln_g[o], c_ln_b[o])
                xb, xn = matmul_residual([y], [W(c_w_out, o)], xb, mix_post_g[l], xa_pre_g[l], 1.0,
                                         tm=TM_TAIL, tn=TN, name="conf_out")

            mn = prenorm(mem_b, xa_mem_g[l])
            k = matmul([mn], [[W(xa_w_k, l)]], _identity, BF16, n=x_w, tm=TM, tn=TN, name="xattn_k")
            v = matmul([mn], [[W(xa_w_v, l)]], _identity, BF16, n=x_w, tm=TM, tn=TN, name="xattn_v")
            q = matmul([xn], [[W(xa_w_q, l)]], _colscale, BF16, n=x_w, tm=TM, tn=TN, vec=xq_scale, name="xattn_q")
            o_att = xattn_core(q, k, v)
            xb, xn = matmul_residual([o_att], [W(xa_w_o, l)], xb, xa_post_g[l], ffn2_pre_g[l], 1.0,
                                     tm=TM_TAIL, tn=TN, name="xattn_o")

            g_next = ffn1_pre_g[l + 1] if l + 1 < depth else None
            xb, xn = ffn(1, l, xn, xb, ffn2_post_g[l], g_next)
        outs.append(xb)
    return jnp.stack(outs, axis=0)
```

```python
import functools
import math
from typing import NamedTuple

import jax
import jax.numpy as jnp
from jax import lax
from jax.experimental import pallas as pl
from jax.experimental.pallas import tpu as pltpu

EPS = 1e-6
CHUNK = 64
A_BLOCK = 128
A_GROUP_DIM = 128
B_HEAD_DIM = 128
X_HEADS = 4
C_KERNEL = 31
C_HALO = 32

VMEM_LIMIT_BYTES = 56 * 1024 * 1024

BF16 = jnp.bfloat16
F32 = jnp.float32
LOG2E = math.log2(math.e)


def _params(*sem):
    return pltpu.CompilerParams(dimension_semantics=sem, vmem_limit_bytes=VMEM_LIMIT_BYTES)


def _mm_body(*refs, n_lhs, n_grp, epilogue, has_vec, n_cast):
    n_rhs = n_lhs * n_grp
    lhs = refs[:n_lhs]
    rhs = refs[n_lhs:n_lhs + n_rhs]
    cast_src = refs[n_lhs + n_rhs + has_vec:n_lhs + n_rhs + has_vec + n_cast]
    o_ref = refs[n_lhs + n_rhs + has_vec + n_cast]
    cast_dst = refs[n_lhs + n_rhs + has_vec + n_cast + 1:]
    accs = []
    for g in range(n_grp):
        acc = None
        for l in range(n_lhs):
            d = jnp.dot(lhs[l][...], rhs[g * n_lhs + l][...], preferred_element_type=F32)
            acc = d if acc is None else acc + d
        accs.append(acc)
    if has_vec:
        accs.append(refs[n_lhs + n_rhs][...])
    o_ref[...] = epilogue(*accs).astype(o_ref.dtype)
    for src, dst in zip(cast_src, cast_dst):
        dst[...] = src[...].astype(dst.dtype)


class W(NamedTuple):
    arr: jax.Array
    layer: int = 0
    row0: int = 0
    col0: int = 0

    def spec(self, k, tn, col_block):
        assert self.row0 % k == 0 and self.col0 % tn == 0
        layer, rb, cb = self.layer, self.row0 // k, self.col0 // tn
        return pl.BlockSpec((None, k, tn), lambda i, j: (layer, rb, cb + col_block(i, j)))


def _as_window(w):
    return w if isinstance(w, W) else W(w[None])


class LayerCast(NamedTuple):
    arr: jax.Array
    layer: int
    row_axis: int


def matmul(lhs, rhs, epilogue, out_dtype, *, n, tm, tn, vec=None, casts=(), name="matmul"):
    n_lhs, n_grp = len(lhs), len(rhs)
    m = lhs[0].shape[0]
    tm, tn = min(tm, m), min(tn, n)
    assert m % tm == 0 and n % tn == 0
    grid = (m // tm, n // tn)
    in_specs = [pl.BlockSpec((tm, a.shape[1]), lambda i, j: (i, 0)) for a in lhs]
    flat_rhs = []
    for grp in rhs:
        assert len(grp) == n_lhs
        for a, w in zip(lhs, grp):
            w = _as_window(w)
            in_specs.append(w.spec(a.shape[1], tn, lambda i, j: j))
            flat_rhs.append(w.arr)
    args = list(lhs) + flat_rhs
    if vec is not None:
        in_specs.append(pl.BlockSpec((1, tn), lambda i, j: (0, j)))
        args.append(vec)
    out_specs = [pl.BlockSpec((tm, tn), lambda i, j: (i, j))]
    out_shape = [jax.ShapeDtypeStruct((m, n), out_dtype)]
    for c in casts:
        _, rows, cols = c.arr.shape
        r_blk, c_blk = rows // grid[c.row_axis], cols // grid[1 - c.row_axis]
        assert rows % grid[c.row_axis] == 0 and cols % grid[1 - c.row_axis] == 0
        assert r_blk % 16 == 0 and c_blk % LANES == 0
        if c.row_axis == 0:
            in_specs.append(pl.BlockSpec((None, r_blk, c_blk), lambda i, j, layer=c.layer: (layer, i, j)))
            out_specs.append(pl.BlockSpec((r_blk, c_blk), lambda i, j: (i, j)))
        else:
            in_specs.append(pl.BlockSpec((None, r_blk, c_blk), lambda i, j, layer=c.layer: (layer, j, i)))
            out_specs.append(pl.BlockSpec((r_blk, c_blk), lambda i, j: (j, i)))
        out_shape.append(jax.ShapeDtypeStruct((rows, cols), BF16))
        args.append(c.arr)
    body = functools.partial(_mm_body, n_lhs=n_lhs, n_grp=n_grp, epilogue=epilogue,
                             has_vec=vec is not None, n_cast=len(casts))
    out = pl.pallas_call(
        body,
        grid=grid,
        in_specs=in_specs,
        out_specs=out_specs,
        out_shape=out_shape,
        compiler_params=_params("parallel", "arbitrary"),
        name=name,
    )(*args)
    return out if casts else out[0]


def _mm_residual_body(*refs, n_lhs, n_col, tn, scale, has_next):
    lhs, rhs = refs[:n_lhs], refs[n_lhs:2 * n_lhs]
    rest = list(refs[2 * n_lhs:])
    gp_ref = rest.pop(0)
    gn_ref = rest.pop(0) if has_next else None
    x_hbm, xo_hbm = rest.pop(0), rest.pop(0)
    xn_hbm = rest.pop(0) if has_next else None
    h_scr, x_buf = rest.pop(0), rest.pop(0)
    xn_buf = rest.pop(0) if has_next else None
    sems = rest.pop(0)
    i, j = pl.program_id(0), pl.program_id(1)
    tm = x_buf.shape[0]
    n_chunk = n_col // 2
    rc = tm // n_chunk

    def hbm_rows(r, q):
        return pl.ds(pl.multiple_of(r * tm + q * rc, rc), rc)

    def x_in(r, q):
        return pltpu.make_async_copy(x_hbm.at[hbm_rows(r, q)], x_buf.at[q * rc:(q + 1) * rc], sems.at[0, q])

    def outs(r, q):
        copies = [pltpu.make_async_copy(x_buf.at[q * rc:(q + 1) * rc], xo_hbm.at[hbm_rows(r, q)], sems.at[1, q])]
        if has_next:
            copies.append(pltpu.make_async_copy(xn_buf.at[q * rc:(q + 1) * rc], xn_hbm.at[hbm_rows(r, q)],
                                                sems.at[2, q]))
        return copies

    @pl.when(j < n_col)
    def _():
        acc = None
        for l in range(n_lhs):
            d = jnp.dot(lhs[l][...], rhs[l][...], preferred_element_type=F32)
            acc = d if acc is None else acc + d
        h_scr[:, pl.ds(pl.multiple_of(j * tn, tn), tn)] = acc

    for q in range(n_chunk):
        @pl.when((j == q) & (i > 0))
        def _(q=q):
            for c in outs(i - 1, q):
                c.start()

        @pl.when(j == n_chunk + q)
        def _(q=q):
            @pl.when(i > 0)
            def _():
                for c in outs(i - 1, q):
                    c.wait()
            x_in(i, q).start()

    @pl.when(j == n_col)
    def _():
        for q in range(n_chunk):
            x_in(i, q).wait()
        x_new = x_buf[...] + _rms(h_scr[...], scale * gp_ref[...])
        x_buf[...] = x_new
        if has_next:
            xn_buf[...] = _rms(x_new, gn_ref[...]).astype(xn_buf.dtype)

        @pl.when(i == pl.num_programs(0) - 1)
        def _():
            for q in range(n_chunk):
                for c in outs(i, q):
                    c.start()
            for q in range(n_chunk):
                for c in outs(i, q):
                    c.wait()


def matmul_residual(lhs, rhs, x, g_post, g_next, scale, *, tm, tn, name):
    n_lhs = len(lhs)
    m, d = x.shape
    tm, tn = min(tm, m), min(tn, d)
    n_col = d // tn
    assert m % tm == 0 and d % tn == 0 and n_col >= 2 and tm % (n_col // 2) == 0
    assert math.frexp(scale)[0] == 0.5, "scale is folded into the gain row, exact only for powers of two"
    has_next = g_next is not None
    in_specs = [pl.BlockSpec((tm, a.shape[1]), lambda i, j: (i, 0)) for a in lhs]
    args = list(lhs)
    for a, w in zip(lhs, rhs):
        w = _as_window(w)
        in_specs.append(w.spec(a.shape[1], tn, lambda i, j: jnp.minimum(j, n_col - 1)))
        args.append(w.arr)
    vec = pl.BlockSpec((1, d), lambda i, j: (0, 0))
    hbm = pl.BlockSpec(memory_space=pl.ANY)
    in_specs += [vec] + ([vec] if has_next else []) + [hbm]
    args += [g_post.reshape(1, d)] + ([g_next.reshape(1, d)] if has_next else []) + [x]
    out_shape = [jax.ShapeDtypeStruct((m, d), F32)] + ([jax.ShapeDtypeStruct((m, d), BF16)] if has_next else [])
    scratch = [pltpu.VMEM((tm, d), F32), pltpu.VMEM((tm, d), F32)]
    scratch += [pltpu.VMEM((tm, d), BF16)] if has_next else []
    scratch += [pltpu.SemaphoreType.DMA((3, n_col // 2))]
    body = functools.partial(_mm_residual_body, n_lhs=n_lhs, n_col=n_col, tn=tn, scale=scale, has_next=has_next)
    out = pl.pallas_call(
        body,
        grid=(m // tm, n_col + 1),
        in_specs=in_specs,
        out_specs=[hbm] * len(out_shape),
        out_shape=out_shape,
        scratch_shapes=scratch,
        compiler_params=_params("arbitrary", "arbitrary"),
        name=name,
    )(*args)
    return (out[0], out[1]) if has_next else (out[0], None)


def _identity(a):
    return a


def _colscale(a, s):
    return a * s


def _swiglu_gate(g, u):
    return g * jax.nn.sigmoid(g) * u


def _glu_gate(a, g):
    return a * jax.nn.sigmoid(g)


def _gelu_tanh(a):
    return jax.nn.gelu(a, approximate=True)


def _rms(x, g):
    return x * lax.rsqrt(jnp.mean(x * x, axis=-1, keepdims=True) + EPS) * g


def _prenorm_body(x_ref, g_ref, o_ref):
    o_ref[...] = _rms(x_ref[...], g_ref[...]).astype(o_ref.dtype)


def prenorm(x, g, *, tm=256):
    m, d = x.shape
    tm = min(tm, m)
    return pl.pallas_call(
        _prenorm_body,
        grid=(m // tm,),
        in_specs=[pl.BlockSpec((tm, d), lambda i: (i, 0)), pl.BlockSpec((1, d), lambda i: (0, 0))],
        out_specs=pl.BlockSpec((tm, d), lambda i: (i, 0)),
        out_shape=jax.ShapeDtypeStruct((m, d), BF16),
        compiler_params=_params("parallel"),
        name="prenorm",
    )(x, g.reshape(1, d))


def _gating_body(u_ref, v_ref, lg_ref, lb_ref, ws_ref, bs_ref, o_ref, *, n_blk, n_grp):
    v = v_ref[...].astype(F32)
    mu = jnp.mean(v, axis=-1, keepdims=True)
    vc = v - mu
    vn = vc * lax.rsqrt(jnp.mean(vc * vc, axis=-1, keepdims=True) + EPS) * lg_ref[...] + lb_ref[...]
    vn = vn.astype(BF16)
    t_pos = lax.broadcasted_iota(jnp.int32, (A_BLOCK, A_BLOCK), 0)
    s_pos = lax.broadcasted_iota(jnp.int32, (A_BLOCK, A_BLOCK), 1)
    mask = (s_pos // CHUNK) <= (t_pos // CHUNK)
    for g in range(n_grp):
        w = jnp.where(mask, ws_ref[g], 0.0).astype(BF16)
        b_col = bs_ref[:, g:g + 1]
        cols = slice(g * A_GROUP_DIM, (g + 1) * A_GROUP_DIM)
        for c in range(n_blk):
            rows = slice(c * A_BLOCK, (c + 1) * A_BLOCK)
            mixed = jnp.dot(w, vn[rows, cols], preferred_element_type=F32) + b_col
            o_ref[rows, cols] = (u_ref[rows, cols].astype(F32) * mixed).astype(o_ref.dtype)


def spatial_gating(uv, ln_g, ln_b, w_s, b_s, *, n_blk=4):
    s_len, two_w = uv.shape
    a_w = two_w // 2
    n_grp = a_w // A_GROUP_DIM
    rows = n_blk * A_BLOCK
    body = functools.partial(_gating_body, n_blk=n_blk, n_grp=n_grp)
    return pl.pallas_call(
        body,
        grid=(s_len // rows,),
        in_specs=[
            pl.BlockSpec((rows, a_w), lambda i: (i, 0)),
            pl.BlockSpec((rows, a_w), lambda i: (i, 1)),
            pl.BlockSpec((1, a_w), lambda i: (0, 0)),
            pl.BlockSpec((1, a_w), lambda i: (0, 0)),
            pl.BlockSpec((n_grp, A_BLOCK, A_BLOCK), lambda i: (0, 0, 0)),
            pl.BlockSpec((A_BLOCK, n_grp), lambda i: (0, 0)),
        ],
        out_specs=pl.BlockSpec((rows, a_w), lambda i: (i, 0)),
        out_shape=jax.ShapeDtypeStruct((s_len, a_w), BF16),
        compiler_params=_params("parallel"),
        name="spatial_gating",
    )(uv, uv, ln_g.reshape(1, a_w), ln_b.reshape(1, a_w), w_s, b_s.T)


def _softplus2(z):
    neg_abs = pltpu.bitcast(pltpu.bitcast(z, jnp.uint32) | jnp.uint32(0x80000000), F32)
    return jnp.maximum(z, 0.0) + jnp.log(1.0 + jnp.exp2(neg_abs)) * LOG2E


def _stickbreak_body(q_ref, k_ref, v_ref, o_ref, acc_ref, a_ref, *, tq, tk):
    i = pl.program_id(1)
    n_sub = tq // tk
    q = q_ref[...]
    neg_upper = jnp.where(lax.broadcasted_iota(jnp.int32, (tk, tk), 1)
                          > lax.broadcasted_iota(jnp.int32, (tk, tk), 0), -1.0, 0.0).astype(BF16)
    acc_ref[...] = jnp.zeros_like(acc_ref)

    def accumulate(js):
        for d, j in enumerate(js):
            v_blk = v_ref[pl.ds(pl.multiple_of(j * tk, tk), tk), :]
            acc_ref[...] += lax.dot_general(v_blk, a_ref[d], (((0,), (0,)), ((), ())),
                                            preferred_element_type=F32)

    def tiles(js, prev_js, carry, masked):
        zs = []
        for j in js:
            k_blk = k_ref[pl.ds(pl.multiple_of(j * tk, tk), tk), :]
            zs.append(lax.dot_general(k_blk, q, (((1,), (1,)), ((), ())), preferred_element_type=F32))
        if prev_js is not None:
            accumulate(prev_js)
        sps, zcs = [], []
        for j, z in zip(js, zs):
            sp = _softplus2(z)
            zc = (z - sp) + carry
            if masked:
                s_pos = j * tk + lax.broadcasted_iota(jnp.int32, (tk, tq), 0)
                t_pos = i * tq + lax.broadcasted_iota(jnp.int32, (tk, tq), 1)
                valid = s_pos < t_pos
                sp = jnp.where(valid, sp, 0.0)
                zc = jnp.where(valid, zc, -1e30)
            carry = carry - jnp.sum(sp, axis=0, keepdims=True)
            sps.append(sp.astype(BF16))
            zcs.append(zc)
        withins = [jnp.dot(neg_upper, sp, preferred_element_type=F32) for sp in sps]
        for d, (zc, w) in enumerate(zip(zcs, withins)):
            a_ref[d] = jnp.exp2(zc + w).astype(BF16)
        return carry

    def group_tiles(g):
        return [g * n_sub + (n_sub - 1 - d) for d in range(n_sub)]

    carry = tiles(group_tiles(i), None, jnp.zeros((1, tq), F32), True)

    def group(jj, carry):
        return tiles(group_tiles(i - 1 - jj), group_tiles(i - jj), carry, False)

    lax.fori_loop(0, i, group, carry)
    accumulate(group_tiles(0))
    o_ref[...] = acc_ref[...].T.astype(o_ref.dtype)


def stickbreak_attention(qkv, n_heads, *, tq=1024, tk=256):
    s_len = qkv.shape[0]
    dh = B_HEAD_DIM
    tq, tk = min(tq, s_len), min(tk, s_len)
    assert s_len % tq == 0 and tq % tk == 0
    body = functools.partial(_stickbreak_body, tq=tq, tk=tk)
    return pl.pallas_call(
        body,
        grid=(n_heads, s_len // tq),
        in_specs=[
            pl.BlockSpec((tq, dh), lambda h, i: (i, h)),
            pl.BlockSpec((s_len, dh), lambda h, i: (0, n_heads + h)),
            pl.BlockSpec((s_len, dh), lambda h, i: (0, 2 * n_heads + h)),
        ],
        out_specs=pl.BlockSpec((tq, dh), lambda h, i: (i, h)),
        out_shape=jax.ShapeDtypeStruct((s_len, n_heads * dh), BF16),
        scratch_shapes=[pltpu.VMEM((dh, tq), F32), pltpu.VMEM((tq // tk, tk, tq), BF16)],
        compiler_params=_params("parallel", "arbitrary"),
        name="stickbreak_attention",
    )(qkv, qkv, qkv)


SUBLANES = 8
LANES = 128


def _conv_body(y_ref, halo_ref, w_ref, b_ref, lg_ref, lb_ref, o_ref, ext_ref, conv_ref, *, tt, row_chunk):
    i = pl.program_id(0)
    width = y_ref.shape[1]
    ext_ref[0:C_HALO, :] = jnp.where(i > 0, halo_ref[...], 0.0)
    ext_ref[C_HALO:, :] = y_ref[...]
    first = C_HALO - (C_KERNEL - 1)
    last = first + C_KERNEL - 1

    def lane_chunk(c, _):
        cols = pl.ds(pl.multiple_of(c * LANES, LANES), LANES)
        bias = b_ref[:, cols]
        for r0 in range(0, tt, row_chunk):
            acc = jnp.zeros((row_chunk, LANES), F32) + bias
            for phase in range(SUBLANES):
                steps = [a for a in range(last // SUBLANES + 1) if first <= a * SUBLANES + phase <= last]
                span = row_chunk + steps[-1] * SUBLANES
                if phase:
                    window = ext_ref[pl.ds(r0, span + SUBLANES), cols]
                    shifted = pltpu.roll(window, span + SUBLANES - phase, axis=0)
                else:
                    shifted = ext_ref[pl.ds(r0, span), cols]
                for a in steps:
                    k = a * SUBLANES + phase - first
                    acc = acc + w_ref[k:k + 1, cols] * shifted[a * SUBLANES:a * SUBLANES + row_chunk]
            conv_ref[r0:r0 + row_chunk, cols] = acc
        return 0

    lax.fori_loop(0, width // LANES, lane_chunk, 0)
    y = conv_ref[...]
    mu = jnp.mean(y, axis=-1, keepdims=True)
    yc = y - mu
    yn = yc * lax.rsqrt(jnp.mean(yc * yc, axis=-1, keepdims=True) + EPS) * lg_ref[...] + lb_ref[...]
    o_ref[...] = (yn * jax.nn.sigmoid(yn)).astype(o_ref.dtype)


def conv_ln_swish(y, w_dw, b_dw, ln_g, ln_b, *, tt=256, row_chunk=128):
    s_len, width = y.shape
    tt = min(tt, s_len)
    halo_per_tile = tt // C_HALO
    vec = pl.BlockSpec((1, width), lambda i: (0, 0))
    return pl.pallas_call(
        functools.partial(_conv_body, tt=tt, row_chunk=row_chunk),
        grid=(s_len // tt,),
        in_specs=[
            pl.BlockSpec((tt, width), lambda i: (i, 0)),
            pl.BlockSpec((C_HALO, width), lambda i: (jnp.maximum(i * halo_per_tile - 1, 0), 0)),
            pl.BlockSpec((C_KERNEL, width), lambda i: (0, 0)),
            vec, vec, vec,
        ],
        out_specs=pl.BlockSpec((tt, width), lambda i: (i, 0)),
        out_shape=jax.ShapeDtypeStruct((s_len, width), BF16),
        scratch_shapes=[pltpu.VMEM((tt + C_HALO, width), F32), pltpu.VMEM((tt, width), F32)],
        compiler_params=_params("parallel"),
        name="conv_ln_swish",
    )(y, y, w_dw, b_dw.reshape(1, width), ln_g.reshape(1, width), ln_b.reshape(1, width))


def _xattn_body(q_ref, k_ref, v_ref, o_ref, *, n_heads):
    dh = q_ref.shape[1] // n_heads
    for h in range(n_heads):
        cols = slice(h * dh, (h + 1) * dh)
        s = lax.dot_general(q_ref[:, cols], k_ref[:, cols], (((1,), (1,)), ((), ())),
                            preferred_element_type=F32)
        e = jnp.exp(s - jnp.max(s, axis=-1, keepdims=True))
        p = e / jnp.sum(e, axis=-1, keepdims=True)
        o_ref[:, cols] = jnp.dot(p.astype(BF16), v_ref[:, cols],
                                 preferred_element_type=F32).astype(o_ref.dtype)


def xattn_core(q, k, v, *, tm=512):
    s_len, width = q.shape
    n_mem = k.shape[0]
    tm = min(tm, s_len)
    return pl.pallas_call(
        functools.partial(_xattn_body, n_heads=X_HEADS),
        grid=(s_len // tm,),
        in_specs=[
            pl.BlockSpec((tm, width), lambda i: (i, 0)),
            pl.BlockSpec((n_mem, width), lambda i: (0, 0)),
            pl.BlockSpec((n_mem, width), lambda i: (0, 0)),
        ],
        out_specs=pl.BlockSpec((tm, width), lambda i: (i, 0)),
        out_shape=jax.ShapeDtypeStruct((s_len, width), BF16),
        compiler_params=_params("parallel"),
        name="xattn_core",
    )(q, k, v)


TM = 1024
TM_TAIL = 512
TN = 1024
TN_PAIR = 512
TN_DOWN = 512


def kernel(x, mem, ffn1_pre_g, ffn1_post_g, ffn1_w_gate, ffn1_w_up, ffn1_w_down, mix_pre_g, mix_post_g, ab_w_in, ab_w_out, a_ln_g, a_ln_b, a_w_s, a_b_s, c_w_in, c_w_dw, c_b_dw, c_ln_g, c_ln_b, c_w_out, xa_pre_g, xa_post_g, xa_mem_g, xa_w_q, xa_w_k, xa_w_v, xa_w_o, ffn2_pre_g, ffn2_post_g, ffn2_w_gate, ffn2_w_up, ffn2_w_down):
    bsz, s_len, d = x.shape
    depth = ffn1_pre_g.shape[0]
    d_ff = ffn1_w_gate.shape[2]
    a_w = a_ln_g.shape[1]
    b_w = ab_w_out.shape[1] - a_w
    b_heads = b_w // B_HEAD_DIM
    c_w = c_w_out.shape[1]
    x_w = xa_w_q.shape[2]
    q_scale = jnp.concatenate([jnp.full((1, b_w), B_HEAD_DIM ** -0.5 * LOG2E, F32), jnp.ones((1, 2 * b_w), F32)], axis=1)
    xq_scale = jnp.full((1, x_w), (x_w // X_HEADS) ** -0.5, F32)

    bf = lambda w: w.astype(BF16)
    ab_w_in, ab_w_out, c_w_in, c_w_out = bf(ab_w_in), bf(ab_w_out), bf(c_w_in), bf(c_w_out)
    xa_w_q, xa_w_k, xa_w_v, xa_w_o = bf(xa_w_q), bf(xa_w_k), bf(xa_w_v), bf(xa_w_o)
    ffn_f32 = [(ffn1_w_gate, ffn1_w_up, ffn1_w_down), (ffn2_w_gate, ffn2_w_up, ffn2_w_down)]
    ffn_bf16 = {(0, 0): tuple(bf(w[0]) for w in ffn_f32[0])}

    def ffn(which, l, xn, xb, g_post, g_next):
        w_gate, w_up, w_down = (W(w[None]) for w in ffn_bf16[(which, l)])
        nxt = (1, l) if which == 0 else (0, l + 1)
        casts = ()
        if nxt[1] < depth and nxt not in ffn_bf16:
            gate, up, down = ffn_f32[nxt[0]]
            casts = (LayerCast(gate, nxt[1], 0), LayerCast(up, nxt[1], 0), LayerCast(down, nxt[1], 1))
        out = matmul([xn], [[w_gate], [w_up]], _swiglu_gate, BF16, n=d_ff, tm=TM, tn=TN_PAIR, casts=casts, name="ffn_in")
        if casts:
            mid, ffn_bf16[nxt] = out[0], tuple(out[1:])
        else:
            mid = out
        return matmul_residual([mid], [w_down], xb, g_post, g_next, 0.5, tm=TM_TAIL, tn=TN_DOWN, name="ffn_down")

    outs = []
    for b in range(bsz):
        xb = x[b]
        mem_b = mem[b]
        xn = prenorm(xb, ffn1_pre_g[0])
        for l in range(depth):
            xb, xn = ffn(0, l, xn, xb, ffn1_post_g[l], mix_pre_g[l])

            if l % 2 == 0:
                e = l // 2
                uv = matmul([xn], [[W(ab_w_in, e)]], _gelu_tanh, BF16, n=2 * a_w, tm=TM, tn=TN, name="ab_in_uv")
                qkv = matmul([xn], [[W(ab_w_in, e, col0=2 * a_w)]], _colscale, BF16, n=3 * b_w, tm=TM, tn=TN,
                             vec=q_scale, name="ab_in_qkv")
                a_out = spatial_gating(uv, a_ln_g[e], a_ln_b[e], a_w_s[e], a_b_s[e])
                b_out = stickbreak_attention(qkv, b_heads)
                xb, xn = matmul_residual([a_out, b_out], [W(ab_w_out, e), W(ab_w_out, e, row0=a_w)], xb,
                                         mix_post_g[l], xa_pre_g[l], 1.0, tm=TM_TAIL, tn=TN, name="ab_out")
            else:
                o = l // 2
                y = matmul([xn], [[W(c_w_in, o)], [W(c_w_in, o, col0=c_w)]], _glu_gate, F32, n=c_w,
                           tm=TM, tn=TN_PAIR, name="conf_in")
                y = conv_ln_swish(y, c_w_dw[o], c_b_dw[o], c_ln_g[o], c_ln_b[o])
                xb, xn = matmul_residual([y], [W(c_w_out, o)], xb, mix_post_g[l], xa_pre_g[l], 1.0,
                                         tm=TM_TAIL, tn=TN, name="conf_out")

            mn = prenorm(mem_b, xa_mem_g[l])
            k = matmul([mn], [[W(xa_w_k, l)]], _identity, BF16, n=x_w, tm=TM, tn=TN, name="xattn_k")
            v = matmul([mn], [[W(xa_w_v, l)]], _identity, BF16, n=x_w, tm=TM, tn=TN, name="xattn_v")
            q = matmul([xn], [[W(xa_w_q, l)]], _colscale, BF16, n=x_w, tm=TM, tn=TN, vec=xq_scale, name="xattn_q")
            o_att = xattn_core(q, k, v)
            xb, xn = matmul_residual([o_att], [W(xa_w_o, l)], xb, xa_post_g[l], ffn2_pre_g[l], 1.0,
                                     tm=TM_TAIL, tn=TN, name="xattn_o")

            g_next = ffn1_pre_g[l + 1] if l + 1 < depth else None
            xb, xn = ffn(1, l, xn, xb, ffn2_post_g[l], g_next)
        outs.append(xb)
    return jnp.stack(outs, axis=0)
```

```python
import functools
import math
from typing import NamedTuple

import jax
import jax.numpy as jnp
from jax import lax
from jax.experimental import pallas as pl
from jax.experimental.pallas import tpu as pltpu

EPS = 1e-6
CHUNK = 64
A_BLOCK = 128
A_GROUP_DIM = 128
B_HEAD_DIM = 128
X_HEADS = 4
C_KERNEL = 31
C_HALO = 32

VMEM_LIMIT_BYTES = 56 * 1024 * 1024

BF16 = jnp.bfloat16
F32 = jnp.float32
LOG2E = math.log2(math.e)


def _params(*sem):
    return pltpu.CompilerParams(dimension_semantics=sem, vmem_limit_bytes=VMEM_LIMIT_BYTES)


def _mm_body(*refs, n_lhs, n_grp, epilogue, has_vec, n_cast):
    n_rhs = n_lhs * n_grp
    lhs = refs[:n_lhs]
    rhs = refs[n_lhs:n_lhs + n_rhs]
    cast_src = refs[n_lhs + n_rhs + has_vec:n_lhs + n_rhs + has_vec + n_cast]
    o_ref = refs[n_lhs + n_rhs + has_vec + n_cast]
    cast_dst = refs[n_lhs + n_rhs + has_vec + n_cast + 1:]
    accs = []
    for g in range(n_grp):
        acc = None
        for l in range(n_lhs):
            d = jnp.dot(lhs[l][...], rhs[g * n_lhs + l][...], preferred_element_type=F32)
            acc = d if acc is None else acc + d
        accs.append(acc)
    if has_vec:
        accs.append(refs[n_lhs + n_rhs][...])
    o_ref[...] = epilogue(*accs).astype(o_ref.dtype)
    for src, dst in zip(cast_src, cast_dst):
        dst[...] = src[...].astype(dst.dtype)


class W(NamedTuple):
    arr: jax.Array
    layer: int = 0
    row0: int = 0
    col0: int = 0

    def spec(self, k, tn, col_block):
        assert self.row0 % k == 0 and self.col0 % tn == 0
        layer, rb, cb = self.layer, self.row0 // k, self.col0 // tn
        return pl.BlockSpec((None, k, tn), lambda i, j: (layer, rb, cb + col_block(i, j)))


def _as_window(w):
    return w if isinstance(w, W) else W(w[None])


class LayerCast(NamedTuple):
    arr: jax.Array
    layer: int
    row_axis: int


def matmul(lhs, rhs, epilogue, out_dtype, *, n, tm, tn, vec=None, casts=(), name="matmul"):
    n_lhs, n_grp = len(lhs), len(rhs)
    m = lhs[0].shape[0]
    tm, tn = min(tm, m), min(tn, n)
    assert m % tm == 0 and n % tn == 0
    grid = (m // tm, n // tn)
    in_specs = [pl.BlockSpec((tm, a.shape[1]), lambda i, j: (i, 0)) for a in lhs]
    flat_rhs = []
    for grp in rhs:
        assert len(grp) == n_lhs
        for a, w in zip(lhs, grp):
            w = _as_window(w)
            in_specs.append(w.spec(a.shape[1], tn, lambda i, j: j))
            flat_rhs.append(w.arr)
    args = list(lhs) + flat_rhs
    if vec is not None:
        in_specs.append(pl.BlockSpec((1, tn), lambda i, j: (0, j)))
        args.append(vec)
    out_specs = [pl.BlockSpec((tm, tn), lambda i, j: (i, j))]
    out_shape = [jax.ShapeDtypeStruct((m, n), out_dtype)]
    for c in casts:
        _, rows, cols = c.arr.shape
        r_blk, c_blk = rows // grid[c.row_axis], cols // grid[1 - c.row_axis]
        assert rows % grid[c.row_axis] == 0 and cols % grid[1 - c.row_axis] == 0
        assert r_blk % 16 == 0 and c_blk % LANES == 0
        if c.row_axis == 0:
            in_specs.append(pl.BlockSpec((None, r_blk, c_blk), lambda i, j, layer=c.layer: (layer, i, j)))
            out_specs.append(pl.BlockSpec((r_blk, c_blk), lambda i, j: (i, j)))
        else:
            in_specs.append(pl.BlockSpec((None, r_blk, c_blk), lambda i, j, layer=c.layer: (layer, j, i)))
            out_specs.append(pl.BlockSpec((r_blk, c_blk), lambda i, j: (j, i)))
        out_shape.append(jax.ShapeDtypeStruct((rows, cols), BF16))
        args.append(c.arr)
    body = functools.partial(_mm_body, n_lhs=n_lhs, n_grp=n_grp, epilogue=epilogue,
                             has_vec=vec is not None, n_cast=len(casts))
    out = pl.pallas_call(
        body,
        grid=grid,
        in_specs=in_specs,
        out_specs=out_specs,
        out_shape=out_shape,
        compiler_params=_params("parallel", "arbitrary"),
        name=name,
    )(*args)
    return out if casts else out[0]


def _mm_residual_body(*refs, n_lhs, n_col, tn, scale, has_next):
    lhs, rhs = refs[:n_lhs], refs[n_lhs:2 * n_lhs]
    rest = list(refs[2 * n_lhs:])
    gp_ref = rest.pop(0)
    gn_ref = rest.pop(0) if has_next else None
    x_hbm, xo_hbm = rest.pop(0), rest.pop(0)
    xn_hbm = rest.pop(0) if has_next else None
    h_scr, x_buf = rest.pop(0), rest.pop(0)
    xn_buf = rest.pop(0) if has_next else None
    sems = rest.pop(0)
    i, j = pl.program_id(0), pl.program_id(1)
    tm = x_buf.shape[0]
    n_chunk = n_col // 2
    rc = tm // n_chunk

    def hbm_rows(r, q):
        return pl.ds(pl.multiple_of(r * tm + q * rc, rc), rc)

    def x_in(r, q):
        return pltpu.make_async_copy(x_hbm.at[hbm_rows(r, q)], x_buf.at[q * rc:(q + 1) * rc], sems.at[0, q])

    def outs(r, q):
        copies = [pltpu.make_async_copy(x_buf.at[q * rc:(q + 1) * rc], xo_hbm.at[hbm_rows(r, q)], sems.at[1, q])]
        if has_next:
            copies.append(pltpu.make_async_copy(xn_buf.at[q * rc:(q + 1) * rc], xn_hbm.at[hbm_rows(r, q)],
                                                sems.at[2, q]))
        return copies

    @pl.when(j < n_col)
    def _():
        acc = None
        for l in range(n_lhs):
            d = jnp.dot(lhs[l][...], rhs[l][...], preferred_element_type=F32)
            acc = d if acc is None else acc + d
        h_scr[:, pl.ds(pl.multiple_of(j * tn, tn), tn)] = acc

    for q in range(n_chunk):
        @pl.when((j == q) & (i > 0))
        def _(q=q):
            for c in outs(i - 1, q):
                c.start()

        @pl.when(j == n_chunk + q)
        def _(q=q):
            @pl.when(i > 0)
            def _():
                for c in outs(i - 1, q):
                    c.wait()
            x_in(i, q).start()

    @pl.when(j == n_col)
    def _():
        for q in range(n_chunk):
            x_in(i, q).wait()
        x_new = x_buf[...] + _rms(h_scr[...], scale * gp_ref[...])
        x_buf[...] = x_new
        if has_next:
            xn_buf[...] = _rms(x_new, gn_ref[...]).astype(xn_buf.dtype)

        @pl.when(i == pl.num_programs(0) - 1)
        def _():
            for q in range(n_chunk):
                for c in outs(i, q):
                    c.start()
            for q in range(n_chunk):
                for c in outs(i, q):
                    c.wait()


def matmul_residual(lhs, rhs, x, g_post, g_next, scale, *, tm, tn, name):
    n_lhs = len(lhs)
    m, d = x.shape
    tm, tn = min(tm, m), min(tn, d)
    n_col = d // tn
    assert m % tm == 0 and d % tn == 0 and n_col >= 2 and tm % (n_col // 2) == 0
    assert math.frexp(scale)[0] == 0.5, "scale is folded into the gain row, exact only for powers of two"
    has_next = g_next is not None
    in_specs = [pl.BlockSpec((tm, a.shape[1]), lambda i, j: (i, 0)) for a in lhs]
    args = list(lhs)
    for a, w in zip(lhs, rhs):
        w = _as_window(w)
        in_specs.append(w.spec(a.shape[1], tn, lambda i, j: jnp.minimum(j, n_col - 1)))
        args.append(w.arr)
    vec = pl.BlockSpec((1, d), lambda i, j: (0, 0))
    hbm = pl.BlockSpec(memory_space=pl.ANY)
    in_specs += [vec] + ([vec] if has_next else []) + [hbm]
    args += [g_post.reshape(1, d)] + ([g_next.reshape(1, d)] if has_next else []) + [x]
    out_shape = [jax.ShapeDtypeStruct((m, d), F32)] + ([jax.ShapeDtypeStruct((m, d), BF16)] if has_next else [])
    scratch = [pltpu.VMEM((tm, d), F32), pltpu.VMEM((tm, d), F32)]
    scratch += [pltpu.VMEM((tm, d), BF16)] if has_next else []
    scratch += [pltpu.SemaphoreType.DMA((3, n_col // 2))]
    body = functools.partial(_mm_residual_body, n_lhs=n_lhs, n_col=n_col, tn=tn, scale=scale, has_next=has_next)
    out = pl.pallas_call(
        body,
        grid=(m // tm, n_col + 1),
        in_specs=in_specs,
        out_specs=[hbm] * len(out_shape),
        out_shape=out_shape,
        scratch_shapes=scratch,
        compiler_params=_params("arbitrary", "arbitrary"),
        name=name,
    )(*args)
    return (out[0], out[1]) if has_next else (out[0], None)


def _identity(a):
    return a


def _colscale(a, s):
    return a * s


def _swiglu_gate(g, u):
    return g * jax.nn.sigmoid(g) * u


def _glu_gate(a, g):
    return a * jax.nn.sigmoid(g)


def _gelu_tanh(a):
    return jax.nn.gelu(a, approximate=True)


def _rms(x, g):
    return x * lax.rsqrt(jnp.mean(x * x, axis=-1, keepdims=True) + EPS) * g


def _prenorm_body(x_ref, g_ref, o_ref):
    o_ref[...] = _rms(x_ref[...], g_ref[...]).astype(o_ref.dtype)


def prenorm(x, g, *, tm=256):
    m, d = x.shape
    tm = min(tm, m)
    return pl.pallas_call(
        _prenorm_body,
        grid=(m // tm,),
        in_specs=[pl.BlockSpec((tm, d), lambda i: (i, 0)), pl.BlockSpec((1, d), lambda i: (0, 0))],
        out_specs=pl.BlockSpec((tm, d), lambda i: (i, 0)),
        out_shape=jax.ShapeDtypeStruct((m, d), BF16),
        compiler_params=_params("parallel"),
        name="prenorm",
    )(x, g.reshape(1, d))


def _gating_body(u_ref, v_ref, lg_ref, lb_ref, ws_ref, bs_ref, o_ref, *, n_blk, n_grp):
    v = v_ref[...].astype(F32)
    mu = jnp.mean(v, axis=-1, keepdims=True)
    vc = v - mu
    vn = vc * lax.rsqrt(jnp.mean(vc * vc, axis=-1, keepdims=True) + EPS) * lg_ref[...] + lb_ref[...]
    vn = vn.astype(BF16)
    t_pos = lax.broadcasted_iota(jnp.int32, (A_BLOCK, A_BLOCK), 0)
    s_pos = lax.broadcasted_iota(jnp.int32, (A_BLOCK, A_BLOCK), 1)
    mask = (s_pos // CHUNK) <= (t_pos // CHUNK)
    for g in range(n_grp):
        w = jnp.where(mask, ws_ref[g], 0.0).astype(BF16)
        b_col = bs_ref[:, g:g + 1]
        cols = slice(g * A_GROUP_DIM, (g + 1) * A_GROUP_DIM)
        for c in range(n_blk):
            rows = slice(c * A_BLOCK, (c + 1) * A_BLOCK)
            mixed = jnp.dot(w, vn[rows, cols], preferred_element_type=F32) + b_col
            o_ref[rows, cols] = (u_ref[rows, cols].astype(F32) * mixed).astype(o_ref.dtype)


def spatial_gating(uv, ln_g, ln_b, w_s, b_s, *, n_blk=4):
    s_len, two_w = uv.shape
    a_w = two_w // 2
    n_grp = a_w // A_GROUP_DIM
    rows = n_blk * A_BLOCK
    body = functools.partial(_gating_body, n_blk=n_blk, n_grp=n_grp)
    return pl.pallas_call(
        body,
        grid=(s_len // rows,),
        in_specs=[
            pl.BlockSpec((rows, a_w), lambda i: (i, 0)),
            pl.BlockSpec((rows, a_w), lambda i: (i, 1)),
            pl.BlockSpec((1, a_w), lambda i: (0, 0)),
            pl.BlockSpec((1, a_w), lambda i: (0, 0)),
            pl.BlockSpec((n_grp, A_BLOCK, A_BLOCK), lambda i: (0, 0, 0)),
            pl.BlockSpec((A_BLOCK, n_grp), lambda i: (0, 0)),
        ],
        out_specs=pl.BlockSpec((rows, a_w), lambda i: (i, 0)),
        out_shape=jax.ShapeDtypeStruct((s_len, a_w), BF16),
        compiler_params=_params("parallel"),
        name="spatial_gating",
    )(uv, uv, ln_g.reshape(1, a_w), ln_b.reshape(1, a_w), w_s, b_s.T)


def _softplus2(z):
    neg_abs = pltpu.bitcast(pltpu.bitcast(z, jnp.uint32) | jnp.uint32(0x80000000), F32)
    return jnp.maximum(z, 0.0) + jnp.log(1.0 + jnp.exp2(neg_abs)) * LOG2E


def _stickbreak_body(q_ref, k_ref, v_ref, o_ref, acc_ref, a_ref, *, tq, tk):
    i = pl.program_id(1)
    n_sub = tq // tk
    q = q_ref[...]
    neg_upper = jnp.where(lax.broadcasted_iota(jnp.int32, (tk, tk), 1)
                          > lax.broadcasted_iota(jnp.int32, (tk, tk), 0), -1.0, 0.0).astype(BF16)
    acc_ref[...] = jnp.zeros_like(acc_ref)

    def accumulate(js):
        for d, j in enumerate(js):
            v_blk = v_ref[pl.ds(pl.multiple_of(j * tk, tk), tk), :]
            acc_ref[...] += lax.dot_general(v_blk, a_ref[d], (((0,), (0,)), ((), ())),
                                            preferred_element_type=F32)

    def tiles(js, prev_js, carry, masked):
        zs = []
        for j in js:
            k_blk = k_ref[pl.ds(pl.multiple_of(j * tk, tk), tk), :]
            zs.append(lax.dot_general(k_blk, q, (((1,), (1,)), ((), ())), preferred_element_type=F32))
        if prev_js is not None:
            accumulate(prev_js)
        sps, zcs = [], []
        for j, z in zip(js, zs):
            sp = _softplus2(z)
            zc = (z - sp) + carry
            if masked:
                s_pos = j * tk + lax.broadcasted_iota(jnp.int32, (tk, tq), 0)
                t_pos = i * tq + lax.broadcasted_iota(jnp.int32, (tk, tq), 1)
                valid = s_pos < t_pos
                sp = jnp.where(valid, sp, 0.0)
                zc = jnp.where(valid, zc, -1e30)
            carry = carry - jnp.sum(sp, axis=0, keepdims=True)
            sps.append(sp.astype(BF16))
            zcs.append(zc)
        withins = [jnp.dot(neg_upper, sp, preferred_element_type=F32) for sp in sps]
        for d, (zc, w) in enumerate(zip(zcs, withins)):
            a_ref[d] = jnp.exp2(zc + w).astype(BF16)
        return carry

    def group_tiles(g):
        return [g * n_sub + (n_sub - 1 - d) for d in range(n_sub)]

    carry = tiles(group_tiles(i), None, jnp.zeros((1, tq), F32), True)

    def group(jj, carry):
        return tiles(group_tiles(i - 1 - jj), group_tiles(i - jj), carry, False)

    lax.fori_loop(0, i, group, carry)
    accumulate(group_tiles(0))
    o_ref[...] = acc_ref[...].T.astype(o_ref.dtype)


def stickbreak_attention(qkv, n_heads, *, tq=512, tk=256):
    s_len = qkv.shape[0]
    dh = B_HEAD_DIM
    tq, tk = min(tq, s_len), min(tk, s_len)
    assert s_len % tq == 0 and tq % tk == 0
    body = functools.partial(_stickbreak_body, tq=tq, tk=tk)
    return pl.pallas_call(
        body,
        grid=(n_heads, s_len // tq),
        in_specs=[
            pl.BlockSpec((tq, dh), lambda h, i: (i, h)),
            pl.BlockSpec((s_len, dh), lambda h, i: (0, n_heads + h)),
            pl.BlockSpec((s_len, dh), lambda h, i: (0, 2 * n_heads + h)),
        ],
        out_specs=pl.BlockSpec((tq, dh), lambda h, i: (i, h)),
        out_shape=jax.ShapeDtypeStruct((s_len, n_heads * dh), BF16),
        scratch_shapes=[pltpu.VMEM((dh, tq), F32), pltpu.VMEM((tq // tk, tk, tq), BF16)],
        compiler_params=_params("parallel", "arbitrary"),
        name="stickbreak_attention",
    )(qkv, qkv, qkv)


SUBLANES = 8
LANES = 128


def _conv_body(y_ref, halo_ref, w_ref, b_ref, lg_ref, lb_ref, o_ref, ext_ref, conv_ref, *, tt, row_chunk):
    i = pl.program_id(0)
    width = y_ref.shape[1]
    ext_ref[0:C_HALO, :] = jnp.where(i > 0, halo_ref[...], 0.0)
    ext_ref[C_HALO:, :] = y_ref[...]
    first = C_HALO - (C_KERNEL - 1)
    last = first + C_KERNEL - 1

    def lane_chunk(c, _):
        cols = pl.ds(pl.multiple_of(c * LANES, LANES), LANES)
        bias = b_ref[:, cols]
        for r0 in range(0, tt, row_chunk):
            acc = jnp.zeros((row_chunk, LANES), F32) + bias
            for phase in range(SUBLANES):
                steps = [a for a in range(last // SUBLANES + 1) if first <= a * SUBLANES + phase <= last]
                span = row_chunk + steps[-1] * SUBLANES
                if phase:
                    window = ext_ref[pl.ds(r0, span + SUBLANES), cols]
                    shifted = pltpu.roll(window, span + SUBLANES - phase, axis=0)
                else:
                    shifted = ext_ref[pl.ds(r0, span), cols]
                for a in steps:
                    k = a * SUBLANES + phase - first
                    acc = acc + w_ref[k:k + 1, cols] * shifted[a * SUBLANES:a * SUBLANES + row_chunk]
            conv_ref[r0:r0 + row_chunk, cols] = acc
        return 0

    lax.fori_loop(0, width // LANES, lane_chunk, 0)
    y = conv_ref[...]
    mu = jnp.mean(y, axis=-1, keepdims=True)
    yc = y - mu
    yn = yc * lax.rsqrt(jnp.mean(yc * yc, axis=-1, keepdims=True) + EPS) * lg_ref[...] + lb_ref[...]
    o_ref[...] = (yn * jax.nn.sigmoid(yn)).astype(o_ref.dtype)


def conv_ln_swish(y, w_dw, b_dw, ln_g, ln_b, *, tt=256, row_chunk=128):
    s_len, width = y.shape
    tt = min(tt, s_len)
    halo_per_tile = tt // C_HALO
    vec = pl.BlockSpec((1, width), lambda i: (0, 0))
    return pl.pallas_call(
        functools.partial(_conv_body, tt=tt, row_chunk=row_chunk),
        grid=(s_len // tt,),
        in_specs=[
            pl.BlockSpec((tt, width), lambda i: (i, 0)),
            pl.BlockSpec((C_HALO, width), lambda i: (jnp.maximum(i * halo_per_tile - 1, 0), 0)),
            pl.BlockSpec((C_KERNEL, width), lambda i: (0, 0)),
            vec, vec, vec,
        ],
        out_specs=pl.BlockSpec((tt, width), lambda i: (i, 0)),
        out_shape=jax.ShapeDtypeStruct((s_len, width), BF16),
        scratch_shapes=[pltpu.VMEM((tt + C_HALO, width), F32), pltpu.VMEM((tt, width), F32)],
        compiler_params=_params("parallel"),
        name="conv_ln_swish",
    )(y, y, w_dw, b_dw.reshape(1, width), ln_g.reshape(1, width), ln_b.reshape(1, width))


def _xattn_body(q_ref, k_ref, v_ref, o_ref, *, n_heads):
    dh = q_ref.shape[1] // n_heads
    for h in range(n_heads):
        cols = slice(h * dh, (h + 1) * dh)
        s = lax.dot_general(q_ref[:, cols], k_ref[:, cols], (((1,), (1,)), ((), ())),
                            preferred_element_type=F32)
        e = jnp.exp(s - jnp.max(s, axis=-1, keepdims=True))
        p = e / jnp.sum(e, axis=-1, keepdims=True)
        o_ref[:, cols] = jnp.dot(p.astype(BF16), v_ref[:, cols],
                                 preferred_element_type=F32).astype(o_ref.dtype)


def xattn_core(q, k, v, *, tm=512):
    s_len, width = q.shape
    n_mem = k.shape[0]
    tm = min(tm, s_len)
    return pl.pallas_call(
        functools.partial(_xattn_body, n_heads=X_HEADS),
        grid=(s_len // tm,),
        in_specs=[
            pl.BlockSpec((tm, width), lambda i: (i, 0)),
            pl.BlockSpec((n_mem, width), lambda i: (0, 0)),
            pl.BlockSpec((n_mem, width), lambda i: (0, 0)),
        ],
        out_specs=pl.BlockSpec((tm, width), lambda i: (i, 0)),
        out_shape=jax.ShapeDtypeStruct((s_len, width), BF16),
        compiler_params=_params("parallel"),
        name="xattn_core",
    )(q, k, v)


TM = 1024
TM_TAIL = 512
TN = 1024
TN_PAIR = 512
TN_DOWN = 512


def kernel(x, mem, ffn1_pre_g, ffn1_post_g, ffn1_w_gate, ffn1_w_up, ffn1_w_down, mix_pre_g, mix_post_g, ab_w_in, ab_w_out, a_ln_g, a_ln_b, a_w_s, a_b_s, c_w_in, c_w_dw, c_b_dw, c_ln_g, c_ln_b, c_w_out, xa_pre_g, xa_post_g, xa_mem_g, xa_w_q, xa_w_k, xa_w_v, xa_w_o, ffn2_pre_g, ffn2_post_g, ffn2_w_gate, ffn2_w_up, ffn2_w_down):
    bsz, s_len, d = x.shape
    depth = ffn1_pre_g.shape[0]
    d_ff = ffn1_w_gate.shape[2]
    a_w = a_ln_g.shape[1]
    b_w = ab_w_out.shape[1] - a_w
    b_heads = b_w // B_HEAD_DIM
    c_w = c_w_out.shape[1]
    x_w = xa_w_q.shape[2]
    q_scale = jnp.concatenate([jnp.full((1, b_w), B_HEAD_DIM ** -0.5 * LOG2E, F32), jnp.ones((1, 2 * b_w), F32)], axis=1)
    xq_scale = jnp.full((1, x_w), (x_w // X_HEADS) ** -0.5, F32)

    bf = lambda w: w.astype(BF16)
    ab_w_in, ab_w_out, c_w_in, c_w_out = bf(ab_w_in), bf(ab_w_out), bf(c_w_in), bf(c_w_out)
    xa_w_q, xa_w_k, xa_w_v, xa_w_o = bf(xa_w_q), bf(xa_w_k), bf(xa_w_v), bf(xa_w_o)
    ffn_f32 = [(ffn1_w_gate, ffn1_w_up, ffn1_w_down), (ffn2_w_gate, ffn2_w_up, ffn2_w_down)]
    ffn_bf16 = {(0, 0): tuple(bf(w[0]) for w in ffn_f32[0])}

    def ffn(which, l, xn, xb, g_post, g_next):
        w_gate, w_up, w_down = (W(w[None]) for w in ffn_bf16[(which, l)])
        nxt = (1, l) if which == 0 else (0, l + 1)
        casts = ()
        if nxt[1] < depth and nxt not in ffn_bf16:
            gate, up, down = ffn_f32[nxt[0]]
            casts = (LayerCast(gate, nxt[1], 0), LayerCast(up, nxt[1], 0), LayerCast(down, nxt[1], 1))
        out = matmul([xn], [[w_gate], [w_up]], _swiglu_gate, BF16, n=d_ff, tm=TM, tn=TN_PAIR, casts=casts, name="ffn_in")
        if casts:
            mid, ffn_bf16[nxt] = out[0], tuple(out[1:])
        else:
            mid = out
        return matmul_residual([mid], [w_down], xb, g_post, g_next, 0.5, tm=TM_TAIL, tn=TN_DOWN, name="ffn_down")

    outs = []
    for b in range(bsz):
        xb = x[b]
        mem_b = mem[b]
        xn = prenorm(xb, ffn1_pre_g[0])
        for l in range(depth):
            xb, xn = ffn(0, l, xn, xb, ffn1_post_g[l], mix_pre_g[l])

            if l % 2 == 0:
                e = l // 2
                uv = matmul([xn], [[W(ab_w_in, e)]], _gelu_tanh, BF16, n=2 * a_w, tm=TM, tn=TN, name="ab_in_uv")
                qkv = matmul([xn], [[W(ab_w_in, e, col0=2 * a_w)]], _colscale, BF16, n=3 * b_w, tm=TM, tn=TN,
                             vec=q_scale, name="ab_in_qkv")
                a_out = spatial_gating(uv, a_ln_g[e], a_ln_b[e], a_w_s[e], a_b_s[e])
                b_out = stickbreak_attention(qkv, b_heads)
                xb, xn = matmul_residual([a_out, b_out], [W(ab_w_out, e), W(ab_w_out, e, row0=a_w)], xb,
                                         mix_post_g[l], xa_pre_g[l], 1.0, tm=TM_TAIL, tn=TN, name="ab_out")
            else:
                o = l // 2
                y = matmul([xn], [[W(c_w_in, o)], [W(c_w_in, o, col0=c_w)]], _glu_gate, F32, n=c_w,
                           tm=TM, tn=TN_PAIR, name="conf_in")
                y = conv_ln_swish(y, c_w_dw[o], c_b_dw[o], c_ln_g[o], c_ln_b[o])
                xb, xn = matmul_residual([y], [W(c_w_out, o)], xb, mix_post_g[l], xa_pre_g[l], 1.0,
                                         tm=TM_TAIL, tn=TN, name="conf_out")

            mn = prenorm(mem_b, xa_mem_g[l])
            k = matmul([mn], [[W(xa_w_k, l)]], _identity, BF16, n=x_w, tm=TM, tn=TN, name="xattn_k")
            v = matmul([mn], [[W(xa_w_v, l)]], _identity, BF16, n=x_w, tm=TM, tn=TN, name="xattn_v")
            q = matmul([xn], [[W(xa_w_q, l)]], _colscale, BF16, n=x_w, tm=TM, tn=TN, vec=xq_scale, name="xattn_q")
            o_att = xattn_core(q, k, v)
            xb, xn = matmul_residual([o_att], [W(xa_w_o, l)], xb, xa_post_g[l], ffn2_pre_g[l], 1.0,
                                     tm=TM_TAIL, tn=TN, name="xattn_o")

            g_next = ffn1_pre_g[l + 1] if l + 1 < depth else None
            xb, xn = ffn(1, l, xn, xb, ffn2_post_g[l], g_next)
        outs.append(xb)
    return jnp.stack(outs, axis=0)
```
